```python
import jax, jax.numpy as jnp
from jax import lax
import numpy as np

D_MODEL = 4096
BATCH = 1
SEQ = 16384
DEPTH = 4
DEC_BATCH = 4
DEC_SEQ = 2048
PAST_LEN = 128

N_MIXERS = 2
N_A_LAYERS = (DEPTH + 1) // 2
N_B_LAYERS = DEPTH // 2
GRID_W = 64
NA_HEADS = 32
NA_HEAD_DIM = D_MODEL // NA_HEADS
NA_MAX_ROWS = 8
NA_KW = 16
HGRN_EXPAND = 128
HGRN_HEADS = D_MODEL // HGRN_EXPAND
HGRN_DK = HGRN_EXPAND
HGRN_DV = D_MODEL // HGRN_HEADS
CHUNK = 64
D_FF = 4 * D_MODEL
RMS_EPS = 1e-6

kernel_name = 'na_hgrn2_hybrid_encoder'


def rmsnorm(x, g):
    xf = x.astype(jnp.float32)
    y = xf * lax.rsqrt(jnp.mean(xf * xf, axis=-1, keepdims=True) + RMS_EPS)
    return (y * g.astype(jnp.float32)).astype(x.dtype)


def neighbourhood_attention(h, w_qkv, rpb, w_o):
    bn, t, _ = h.shape
    rows = t // GRID_W
    kh = min(NA_MAX_ROWS, rows)
    qkv = h @ w_qkv
    q, k, v = jnp.split(qkv, 3, axis=-1)
    grid = lambda a: a.reshape(bn, rows, GRID_W, NA_HEADS, NA_HEAD_DIM)
    q, k, v = grid(q), grid(k), grid(v)
    col_start = np.clip(np.arange(GRID_W) - NA_KW // 2, 0, GRID_W - NA_KW)
    col_idx = col_start[:, None] + np.arange(NA_KW)[None, :]
    col_off = jnp.asarray(col_idx - np.arange(GRID_W)[:, None] + (NA_KW - 1))
    col_idx = jnp.asarray(col_idx)
    scale = NA_HEAD_DIM ** -0.5

    def row_block(r):
        rs = jnp.clip(r - kh // 2, 0, rows - kh)
        q_r = lax.dynamic_index_in_dim(q, r, axis=1, keepdims=False)
        k_blk = lax.dynamic_slice_in_dim(k, rs, kh, axis=1)
        v_blk = lax.dynamic_slice_in_dim(v, rs, kh, axis=1)
        k_sel = k_blk[:, :, col_idx]
        v_sel = v_blk[:, :, col_idx]
        row_off = rs + jnp.arange(kh) - r + (NA_MAX_ROWS - 1)
        bias = rpb[:, row_off[None, :, None], col_off[:, None, :]]
        s = jnp.einsum('bqhd,brqjhd->bhqrj', q_r, k_sel).astype(jnp.float32) * scale
        s = s + bias[None].astype(jnp.float32)
        p = jax.nn.softmax(s.reshape(bn, NA_HEADS, GRID_W, kh * NA_KW), axis=-1)
        p = p.reshape(bn, NA_HEADS, GRID_W, kh, NA_KW).astype(v.dtype)
        return jnp.einsum('bhqrj,brqjhd->bqhd', p, v_sel)

    o = lax.map(row_block, jnp.arange(rows, dtype=jnp.int32))
    o = jnp.transpose(o, (1, 0, 2, 3, 4)).reshape(bn, t, D_MODEL)
    return o @ w_o


def gated_linear_scan(q, k, v, log_f):
    n, t, nh, dk = q.shape
    dv = v.shape[-1]
    nc = t // CHUNK
    to_chunks = lambda a: a.reshape(n, nc, CHUNK, nh, a.shape[-1]).transpose(1, 0, 2, 3, 4)
    tril = jnp.tril(jnp.ones((CHUNK, CHUNK), dtype=bool))[None, :, :, None, None]

    def step(state, xs):
        qc, kc, vc, lc = xs
        b = jnp.cumsum(lc, axis=1)
        diff = b[:, :, None] - b[:, None, :]
        decay = jnp.exp(jnp.where(tril, diff, -jnp.inf))
        attn = jnp.einsum('nthk,ntshk,nshk->nhts', qc, decay, kc)
        o = jnp.einsum('nhts,nshv->nthv', attn, vc)
        o = o + jnp.einsum('nthk,nhkv->nthv', qc * jnp.exp(b), state)
        b_last = b[:, -1]
        state = state * jnp.exp(b_last)[..., None] + jnp.einsum(
            'nshk,nshv->nhkv', kc * jnp.exp(b_last[:, None] - b), vc)
        return state, o

    s0 = jnp.zeros((n, nh, dk, dv), jnp.float32)
    _, o = lax.scan(step, s0, (to_chunks(q), to_chunks(k), to_chunks(v), to_chunks(log_f)))
    return o.transpose(1, 0, 2, 3, 4).reshape(n, t, nh, dv)


def hgrn2_mixer(h, w_in, lb, g_gain, w_o):
    bn, t, _ = h.shape
    proj = (h @ w_in).astype(jnp.float32)
    q, i, zf, zb, g = jnp.split(proj, 5, axis=-1)
    heads = lambda a, d: a.reshape(bn, t, HGRN_HEADS, d)
    lb = lb.reshape(HGRN_HEADS, HGRN_DK)
    log_lb, log_1mlb = jnp.log(lb), jnp.log1p(-lb)

    def gates(z):
        z = heads(z, HGRN_DK)
        log_f = jnp.logaddexp(log_lb, log_1mlb + jax.nn.log_sigmoid(z))
        key = (1.0 - lb) * jax.nn.sigmoid(-z)
        return log_f, key

    lf_f, k_f = gates(zf)
    lf_b, k_b = gates(zb)
    q = heads(q, HGRN_DK)
    v = heads(i, HGRN_DV)
    rev = lambda a: a[:, ::-1]
    qd = jnp.concatenate([q, rev(q)], axis=0)
    kd = jnp.concatenate([k_f, rev(k_b)], axis=0)
    vd = jnp.concatenate([v, rev(v)], axis=0)
    ld = jnp.concatenate([lf_f, rev(lf_b)], axis=0)
    o = gated_linear_scan(qd, kd, vd, ld)
    o = o[:bn] + rev(o[bn:])
    o = rmsnorm(o, g_gain) * jax.nn.silu(heads(g, HGRN_DV))
    return o.reshape(bn, t, D_MODEL).astype(h.dtype) @ w_o


def sq_relu_mlp(h, w_up, w_down):
    u = jax.nn.relu(h @ w_up)
    return (u * u) @ w_down


def trunk(x, ln_mix, ln_mlp, ln_final, w_qkv, rpb, w_o_a, w_in_b, lb_sched, g_norm, w_o_b, w_up, w_down):
    for layer in range(DEPTH):
        h = rmsnorm(x, ln_mix[layer])
        j = layer // N_MIXERS
        if layer % N_MIXERS == 0:
            x = x + neighbourhood_attention(h, w_qkv[j], rpb[j], w_o_a[j])
        else:
            x = x + hgrn2_mixer(h, w_in_b[j], lb_sched[layer], g_norm[j], w_o_b[j])
        x = x + sq_relu_mlp(rmsnorm(x, ln_mlp[layer]), w_up[layer], w_down[layer])
    return rmsnorm(x, ln_final)


def setup_inputs(seed: int = 0) -> dict:
    key = jax.random.key(seed)
    ks = jax.random.split(key, 14)
    nrm = lambda k, shape, s: jax.random.normal(k, shape, jnp.float32) * s
    D = D_MODEL
    return {
        'x_prompt': nrm(ks[0], (BATCH, SEQ, D), 1.0),
        'x_sample': nrm(ks[1], (DEC_BATCH, DEC_SEQ, D), 1.0),
        'ln_mix': 1.0 + nrm(ks[2], (DEPTH, D), 0.02),
        'ln_mlp': 1.0 + nrm(ks[3], (DEPTH, D), 0.02),
        'ln_final': 1.0 + nrm(ks[4], (D,), 0.02),
        'w_qkv': nrm(ks[5], (N_A_LAYERS, D, 3 * D), D ** -0.5),
        'rpb': nrm(ks[6], (N_A_LAYERS, NA_HEADS, 2 * NA_MAX_ROWS - 1, 2 * NA_KW - 1), 0.1),
        'w_o_a': nrm(ks[7], (N_A_LAYERS, D, D), D ** -0.5),
        'w_in_b': nrm(ks[8], (N_B_LAYERS, D, 5 * D), D ** -0.5),
        'lower_bounds': 1.0 + nrm(ks[9], (DEPTH, D), 0.1),
        'g_norm': 1.0 + nrm(ks[10], (N_B_LAYERS, HGRN_DV), 0.02),
        'w_o_b': nrm(ks[11], (N_B_LAYERS, D, D), D ** -0.5),
        'w_up': nrm(ks[12], (DEPTH, D, D_FF), D ** -0.5),
        'w_down': nrm(ks[13], (DEPTH, D_FF, D), D_FF ** -0.5),
    }


def reference(x_prompt, x_sample, ln_mix, ln_mlp, ln_final, w_qkv, rpb, w_o_a, w_in_b, lower_bounds, g_norm, w_o_b, w_up, w_down):
    lb_sched = jnp.cumsum(jax.nn.softmax(lower_bounds.astype(jnp.float32), axis=0), axis=0)
    lb_sched = lb_sched - lb_sched[0]
    y_prompt = trunk(x_prompt, ln_mix, ln_mlp, ln_final, w_qkv, rpb, w_o_a, w_in_b, lb_sched, g_norm, w_o_b, w_up, w_down)
    y_sample = trunk(x_sample, ln_mix, ln_mlp, ln_final, w_qkv, rpb, w_o_a, w_in_b, lb_sched, g_norm, w_o_b, w_up, w_down)
    return (y_prompt, y_sample)
```

```python
import functools

import numpy as np
import jax
import jax.numpy as jnp
from jax import lax
from jax.experimental import pallas as pl
from jax.experimental.pallas import tpu as pltpu

HEAD = 128
GRID_W = 64
NA_ROWS = 8
NA_KW = 16
Q_ROWS = 8
K_ROWS = 16
SCAN_CHUNK = 128
SCAN_LEVELS = 7
SCAN_HEADS = 4
RMS_EPS = 1e-6
NEG = -1e30
VMEM_LIMIT = 56 * 1024 * 1024

_NT = (((1,), (1,)), ((), ()))
_TN = (((0,), (0,)), ((), ()))


def _params(*sem):
    return pltpu.CompilerParams(dimension_semantics=sem, vmem_limit_bytes=VMEM_LIMIT)


def _tile(n, pref):
    t = min(n, pref)
    while n % t:
        t //= 2
    return t


def _lb_kernel(x_ref, loglb_ref, log1m_ref, onem_ref):
    x = x_ref[...]
    e = jnp.exp(x - jnp.max(x, axis=0, keepdims=True))
    sm = e / jnp.sum(e, axis=0, keepdims=True)
    depth = x.shape[0]
    c = sm[0:1]
    c0 = c
    for r in range(depth):
        if r:
            c = c + sm[r:r + 1]
        lb = c - c0
        loglb_ref[r:r + 1, :] = jnp.log(lb)
        log1m_ref[r:r + 1, :] = jnp.log1p(-lb)
        onem_ref[r:r + 1, :] = 1.0 - lb


def _lb_schedule(lower_bounds):
    depth, d = lower_bounds.shape
    out = jax.ShapeDtypeStruct((depth, d), jnp.float32)
    return pl.pallas_call(_lb_kernel, out_shape=(out, out, out), name="lb_schedule")(
        lower_bounds.astype(jnp.float32))


def _rmsnorm_kernel(x_ref, g_ref, o_ref):
    x = x_ref[...]
    y = x * lax.rsqrt(jnp.mean(x * x, axis=-1, keepdims=True) + RMS_EPS)
    o_ref[...] = (y * g_ref[...]).astype(o_ref.dtype)


def _rmsnorm(x, g, out_dtype, row0=0, rows=None):
    m, d = x.shape
    rows = m if rows is None else rows
    tm = _tile(rows, 256)
    assert row0 % tm == 0
    off = row0 // tm
    return pl.pallas_call(
        _rmsnorm_kernel,
        grid=(rows // tm,),
        in_specs=[pl.BlockSpec((tm, d), lambda i: (i + off, 0)),
                  pl.BlockSpec((1, d), lambda i: (0, 0))],
        out_specs=pl.BlockSpec((tm, d), lambda i: (i, 0)),
        out_shape=jax.ShapeDtypeStruct((rows, d), out_dtype),
        compiler_params=_params("parallel"),
        name="rmsnorm",
    )(x, g.reshape(1, d).astype(jnp.float32))


def _mm_kernel(a_ref, w_ref, o_ref, *, relu2):
    acc = jnp.dot(a_ref[...], w_ref[...], preferred_element_type=jnp.float32)
    if relu2:
        acc = jnp.maximum(acc, 0.0)
        acc = acc * acc
    o_ref[...] = acc.astype(o_ref.dtype)


def _mm_res_kernel(a_ref, w_ref, r_ref, o_ref):
    k = pl.program_id(2)
    acc = jnp.dot(a_ref[...], w_ref[...], preferred_element_type=jnp.float32)

    @pl.when(k == 0)
    def _():
        o_ref[...] = r_ref[...] + acc

    @pl.when(k != 0)
    def _():
        o_ref[...] += acc


def _matmul(a, w, out_dtype, relu2=False):
    m, kd = a.shape
    n = w.shape[1]
    tm, tn = _tile(m, 1024), _tile(n, 1024)
    return pl.pallas_call(
        functools.partial(_mm_kernel, relu2=relu2),
        grid=(m // tm, n // tn),
        in_specs=[pl.BlockSpec((tm, kd), lambda i, j: (i, 0)),
                  pl.BlockSpec((kd, tn), lambda i, j: (0, j))],
        out_specs=pl.BlockSpec((tm, tn), lambda i, j: (i, j)),
        out_shape=jax.ShapeDtypeStruct((m, n), out_dtype),
        compiler_params=_params("parallel", "parallel"),
        name="matmul_relu2" if relu2 else "matmul",
    )(a, w)


def _matmul_residual(a, w, res):
    m, kd = a.shape
    n = w.shape[1]
    tm, tn, tk = _tile(m, 1024), _tile(n, 1024), _tile(kd, 2048)
    return pl.pallas_call(
        _mm_res_kernel,
        grid=(m // tm, n // tn, kd // tk),
        in_specs=[pl.BlockSpec((tm, tk), lambda i, j, k: (i, k)),
                  pl.BlockSpec((tk, tn), lambda i, j, k: (k, j)),
                  pl.BlockSpec((tm, tn), lambda i, j, k: (i, j))],
        out_specs=pl.BlockSpec((tm, tn), lambda i, j, k: (i, j)),
        out_shape=jax.ShapeDtypeStruct((m, n), jnp.float32),
        input_output_aliases={2: 0},
        compiler_params=_params("parallel", "parallel", "arbitrary"),
        name="matmul_residual",
    )(a, w, res)


def _na_bias_table(rpb):
    h = rpb.shape[0]
    w = GRID_W
    col = np.arange(w)
    cs = np.clip(col - NA_KW // 2, 0, w - NA_KW)
    dcol = col[None, :] - col[:, None] + (NA_KW - 1)
    col_ok = (col[None, :] >= cs[:, None]) & (col[None, :] < cs[:, None] + NA_KW)
    per_row = jnp.where(jnp.asarray(col_ok)[None, None],
                        rpb.astype(jnp.float32)[:, :, np.clip(dcol, 0, 2 * NA_KW - 2)], NEG)
    ri = np.arange(Q_ROWS)[:, None]
    kj = np.arange(K_ROWS)[None, :]
    rel_start = {0: np.maximum(ri - NA_ROWS // 2, 0), 1: ri + 0, 2: np.minimum(ri + NA_ROWS // 2, NA_ROWS)}
    key_off = {0: 0, 1: -(NA_ROWS // 2), 2: -NA_ROWS}
    blocks = []
    for case in range(3):
        row_ok = (kj >= rel_start[case]) & (kj < rel_start[case] + NA_ROWS)
        ro = np.clip(kj + key_off[case] - ri + (NA_ROWS - 1), 0, 2 * NA_ROWS - 2)
        vals = per_row[:, ro]
        vals = jnp.where(jnp.asarray(row_ok)[None, :, :, None, None], vals, NEG)
        blocks.append(vals.transpose(0, 1, 3, 2, 4).reshape(h, Q_ROWS * w, K_ROWS * w))
    return jnp.stack(blocks)


def _na_kernel(q_ref, k_ref, v_ref, bias_ref, o_ref, *, rows, scale):
    b = pl.program_id(2)
    ks = jnp.clip(b * Q_ROWS - NA_ROWS // 2, 0, rows - K_ROWS)
    tok0 = pl.multiple_of(ks * GRID_W, GRID_W * (NA_ROWS // 2))
    k = k_ref[pl.ds(tok0, K_ROWS * GRID_W), :]
    v = v_ref[pl.ds(tok0, K_ROWS * GRID_W), :]
    slab = 2 * GRID_W
    for i in range(Q_ROWS * GRID_W // slab):
        sl = slice(i * slab, (i + 1) * slab)
        s = lax.dot_general(q_ref[sl, :], k, _NT, preferred_element_type=jnp.float32)
        s = s * scale + bias_ref[sl, :]
        e = jnp.exp(s - jnp.max(s, axis=-1, keepdims=True))
        l = jnp.sum(e, axis=-1, keepdims=True)
        o = jnp.dot(e.astype(jnp.bfloat16), v, preferred_element_type=jnp.float32)
        o_ref[sl, :] = (o / l).astype(o_ref.dtype)


def _na_attention(qkv, bias_tab, tok0, n_seq, t):
    d = qkv.shape[1] // 3
    nh = d // HEAD
    rows = t // GRID_W
    assert t % GRID_W == 0 and rows % Q_ROWS == 0 and rows >= K_ROWS and tok0 % t == 0
    nb = rows // Q_ROWS
    tq = Q_ROWS * GRID_W
    s0, qb0 = tok0 // t, tok0 // tq

    def case(b):
        return jnp.where(b == 0, 0, jnp.where(b == nb - 1, 2, 1))

    return pl.pallas_call(
        functools.partial(_na_kernel, rows=rows, scale=HEAD ** -0.5),
        grid=(n_seq, nh, nb),
        in_specs=[pl.BlockSpec((tq, HEAD), lambda s, h, b: (qb0 + s * nb + b, h)),
                  pl.BlockSpec((t, HEAD), lambda s, h, b: (s0 + s, nh + h)),
                  pl.BlockSpec((t, HEAD), lambda s, h, b: (s0 + s, 2 * nh + h)),
                  pl.BlockSpec((None, None, tq, K_ROWS * GRID_W), lambda s, h, b: (case(b), h, 0, 0))],
        out_specs=pl.BlockSpec((tq, HEAD), lambda s, h, b: (s * nb + b, h)),
        out_shape=jax.ShapeDtypeStruct((n_seq * t, d), jnp.bfloat16),
        compiler_params=_params("parallel", "parallel", "arbitrary"),
        name="na_attention",
    )(qkv, qkv, qkv, bias_tab)


def _split3(x):
    hi = x.astype(jnp.bfloat16)
    r1 = x - hi.astype(jnp.float32)
    mid = r1.astype(jnp.bfloat16)
    lo = (r1 - mid.astype(jnp.float32)).astype(jnp.bfloat16)
    return hi, mid, lo


def _scan_kernel(q_ref, v_ref, z_ref, loglb_ref, log1m_ref, onem_ref, *rest, reverse, final):
    if final:
        g_ref, of_ref, gain_ref, o_ref, b_ref, st_ref = rest
    else:
        o_ref, b_ref, st_ref = rest
    c = SCAN_CHUNK
    step = pl.program_id(2)

    @pl.when(step == 0)
    def _():
        st_ref[...] = jnp.zeros_like(st_ref)

    z = z_ref[...]
    e = jnp.exp(-jnp.abs(z))
    log_sig = jnp.minimum(z, 0.0) - jnp.log1p(e)
    a = loglb_ref[...]
    bb = log1m_ref[...] + log_sig
    log_f = jnp.maximum(a, bb) + jnp.log1p(jnp.exp(-jnp.abs(a - bb)))
    key = onem_ref[...] * (jnp.where(z >= 0.0, e, 1.0) / (1.0 + e))

    ti = lax.broadcasted_iota(jnp.int32, (c, c), 0)
    si = lax.broadcasted_iota(jnp.int32, (c, c), 1)
    tri = (si >= ti) if reverse else (si <= ti)
    tri = jnp.where(tri, 1.0, 0.0).astype(jnp.bfloat16)
    hi, mid, lo = _split3(log_f)
    b_all = (jnp.dot(tri, hi, preferred_element_type=jnp.float32)
             + jnp.dot(tri, mid, preferred_element_type=jnp.float32)
             + jnp.dot(tri, lo, preferred_element_type=jnp.float32))
    b_ref[...] = b_all

    x = ti ^ si
    lev = jnp.zeros((c, c), jnp.int32)
    for l in range(1, SCAN_LEVELS):
        lev = lev + jnp.where(x >= (1 << l), 1, 0)
    earlier = (si > ti) if reverse else (si < ti)
    lev = jnp.where(earlier, lev, -1)
    row = lax.broadcasted_iota(jnp.int32, (c, HEAD), 0)
    sub = lax.broadcasted_iota(jnp.int32, (8, HEAD), 0)
    last = 0 if reverse else c - 1

    def brow(r, cols):
        return jnp.broadcast_to(b_ref[r:r + 1, cols], (8, HEAD))

    for h in range(SCAN_HEADS):
        cols = slice(h * HEAD, (h + 1) * HEAD)
        q = q_ref[:, cols]
        v = v_ref[:, cols]
        kk = key[:, cols]
        b = b_all[:, cols]
        attn = jnp.zeros((c, c), jnp.float32)
        for l in range(SCAN_LEVELS):
            half = 1 << l
            pieces = []
            for g in range(c // 8):
                if l >= 3:
                    blk = (g * 8) >> (l + 1) << (l + 1)
                    pieces.append(brow(blk + (half if reverse else half - 1), cols))
                else:
                    n = 8 >> (l + 1)
                    cand = [brow(g * 8 + (j << (l + 1)) + (half if reverse else half - 1), cols)
                            for j in range(n)]
                    p = cand[-1]
                    for j in range(n - 2, -1, -1):
                        p = jnp.where(sub < ((j + 1) << (l + 1)), cand[j], p)
                    pieces.append(p)
            bound = jnp.concatenate(pieces, axis=0)
            later = ((row >> l) & 1) == (0 if reverse else 1)
            dlt = b - bound
            fac = jnp.exp(jnp.where(later, dlt, -dlt))
            xl = (jnp.where(later, q, kk) * fac).astype(jnp.bfloat16)
            zl = lax.dot_general(xl, xl, _NT, preferred_element_type=jnp.float32)
            attn = jnp.where(lev == l, zl, attn)
        o = jnp.dot(attn.astype(jnp.bfloat16), v.astype(jnp.bfloat16), preferred_element_type=jnp.float32)
        o = o + jnp.sum(q * kk, axis=-1, keepdims=True) * v
        st = st_ref[h]
        qs = (q * jnp.exp(b)).astype(jnp.bfloat16)
        o = o + lax.dot_general(qs, st.astype(jnp.bfloat16), _NT, preferred_element_type=jnp.float32)
        b_last = b_ref[last:last + 1, cols]
        kh = (kk * jnp.exp(b_last - b)).astype(jnp.bfloat16)
        st_ref[h] = st * jnp.exp(b_last) + lax.dot_general(
            v.astype(jnp.bfloat16), kh, _TN, preferred_element_type=jnp.float32)
        if final:
            o = o + of_ref[:, cols]
            o = o * lax.rsqrt(jnp.mean(o * o, axis=-1, keepdims=True) + RMS_EPS) * gain_ref[...]
            gt = g_ref[:, cols]
            o = o * (gt / (1.0 + jnp.exp(-gt)))
        o_ref[:, cols] = o.astype(o_ref.dtype)


def _scan(proj, lbs, o_fwd, gain, tok0, n_seq, t, out_rows0):
    d = proj.shape[1] // 5
    c, gw = SCAN_CHUNK, SCAN_HEADS * HEAD
    assert t % c == 0 and tok0 % c == 0 and d % gw == 0
    nc, ng, cb0 = t // c, d // gw, tok0 // c
    final = o_fwd is not None
    ob0 = out_rows0 // c

    def rows(s, ci):
        return s * nc + (nc - 1 - ci if final else ci)

    def col(which):
        return lambda s, g, ci: (cb0 + rows(s, ci), which * ng + g)

    vec = pl.BlockSpec((1, gw), lambda s, g, ci: (0, g))
    in_specs = [pl.BlockSpec((c, gw), col(0)), pl.BlockSpec((c, gw), col(1)),
                pl.BlockSpec((c, gw), col(3 if final else 2)), vec, vec, vec]
    args = [proj, proj, proj, *lbs]
    if final:
        in_specs += [pl.BlockSpec((c, gw), col(4)),
                     pl.BlockSpec((c, gw), lambda s, g, ci: (ob0 + rows(s, ci), g)),
                     pl.BlockSpec((1, HEAD), lambda s, g, ci: (0, 0))]
        args += [proj, o_fwd, gain]
    return pl.pallas_call(
        functools.partial(_scan_kernel, reverse=final, final=final),
        grid=(n_seq, ng, nc),
        in_specs=in_specs,
        out_specs=pl.BlockSpec((c, gw), lambda s, g, ci: (rows(s, ci), g)),
        out_shape=jax.ShapeDtypeStruct((n_seq * t, d), jnp.bfloat16 if final else jnp.float32),
        scratch_shapes=[pltpu.VMEM((c, gw), jnp.float32),
                        pltpu.VMEM((SCAN_HEADS, HEAD, HEAD), jnp.float32)],
        compiler_params=_params("parallel", "parallel", "arbitrary"),
        name="hgrn_scan_bwd" if final else "hgrn_scan_fwd",
    )(*args)


def kernel(x_prompt, x_sample, ln_mix, ln_mlp, ln_final, w_qkv, rpb, w_o_a, w_in_b, lower_bounds, g_norm,
           w_o_b, w_up, w_down):
    d = x_prompt.shape[-1]
    depth = ln_mix.shape[0]
    groups = []
    tok = 0
    for a in (x_prompt, x_sample):
        groups.append((tok, a.shape[0], a.shape[1]))
        tok += a.shape[0] * a.shape[1]
    x = jnp.concatenate([x_prompt.reshape(-1, d), x_sample.reshape(-1, d)], axis=0)

    bf = lambda w: w.astype(jnp.bfloat16)
    loglb, log1m, onem = _lb_schedule(lower_bounds)

    for layer in range(depth):
        j = layer // 2
        h = _rmsnorm(x, ln_mix[layer], jnp.bfloat16)
        if layer % 2 == 0:
            qkv = _matmul(h, bf(w_qkv[j]), jnp.bfloat16)
            bias_tab = _na_bias_table(rpb[j])
            o = jnp.concatenate([_na_attention(qkv, bias_tab, *g) for g in groups], axis=0)
            x = _matmul_residual(o, bf(w_o_a[j]), x)
        else:
            proj = _matmul(h, bf(w_in_b[j]), jnp.float32)
            lbs = [a[layer].reshape(1, d) for a in (loglb, log1m, onem)]
            gain = g_norm[j].reshape(1, HEAD).astype(jnp.float32)
            outs = []
            for g in groups:
                o_fwd = _scan(proj, lbs, None, None, *g, 0)
                outs.append(_scan(proj, lbs, o_fwd, gain, *g, 0))
            o = jnp.concatenate(outs, axis=0)
            x = _matmul_residual(o, bf(w_o_b[j]), x)
        u = _matmul(_rmsnorm(x, ln_mlp[layer], jnp.bfloat16), bf(w_up[layer]), jnp.bfloat16, relu2=True)
        x = _matmul_residual(u, bf(w_down[layer]), x)

    outs = []
    for (tok0, n_seq, t), a in zip(groups, (x_prompt, x_sample)):
        y = _rmsnorm(x, ln_final, jnp.float32, row0=tok0, rows=n_seq * t)
        outs.append(y.reshape(a.shape))
    return tuple(outs)
```

```python
import functools
import math

import numpy as np
import jax
import jax.numpy as jnp
from jax import lax
from jax.experimental import pallas as pl
from jax.experimental.pallas import tpu as pltpu

HEAD = 128
SUBLANES = 8
GRID_W = 64
NA_ROWS = 8
NA_KW = 16
NA_UNROLL = 16
SCAN_CHUNK = 128
SCAN_LEVELS = 7
SMALL_LEVELS = 3
SCAN_HEADS = 8
SAFE_DECAY = 80.0
RMS_EPS = 1e-6
NEG = -1e30
LOG2E = math.log2(math.e)
VMEM_LIMIT = 56 * 1024 * 1024

_NT = (((1,), (1,)), ((), ()))
_TN = (((0,), (0,)), ((), ()))


def _params(*sem):
    return pltpu.CompilerParams(dimension_semantics=sem, vmem_limit_bytes=VMEM_LIMIT)


def _tile(n, pref):
    t = min(n, pref)
    while n % t:
        t //= 2
    return t


def _lb_kernel(x_ref, loglb_ref, log1m_ref, onem_ref):
    x = x_ref[...]
    e = jnp.exp(x - jnp.max(x, axis=0, keepdims=True))
    sm = e / jnp.sum(e, axis=0, keepdims=True)
    depth = x.shape[0]
    c = sm[0:1]
    c0 = c
    for r in range(depth):
        if r:
            c = c + sm[r:r + 1]
        lb = c - c0
        loglb_ref[r:r + 1, :] = jnp.log(lb)
        log1m_ref[r:r + 1, :] = jnp.log1p(-lb)
        onem_ref[r:r + 1, :] = 1.0 - lb


def _lb_schedule(lower_bounds):
    depth, d = lower_bounds.shape
    out = jax.ShapeDtypeStruct((depth, d), jnp.float32)
    return pl.pallas_call(_lb_kernel, out_shape=(out, out, out), name="lb_schedule")(
        lower_bounds.astype(jnp.float32))


def _rmsnorm_kernel(x_ref, g_ref, o_ref):
    x = x_ref[...]
    y = x * lax.rsqrt(jnp.mean(x * x, axis=-1, keepdims=True) + RMS_EPS)
    o_ref[...] = (y * g_ref[...]).astype(o_ref.dtype)


def _rmsnorm(x, g, out_dtype, row0=0, rows=None):
    m, d = x.shape
    rows = m if rows is None else rows
    tm = _tile(rows, 256)
    assert row0 % tm == 0
    off = row0 // tm
    return pl.pallas_call(
        _rmsnorm_kernel,
        grid=(rows // tm,),
        in_specs=[pl.BlockSpec((tm, d), lambda i: (i + off, 0)),
                  pl.BlockSpec((1, d), lambda i: (0, 0))],
        out_specs=pl.BlockSpec((tm, d), lambda i: (i, 0)),
        out_shape=jax.ShapeDtypeStruct((rows, d), out_dtype),
        compiler_params=_params("parallel"),
        name="rmsnorm",
    )(x, g.reshape(1, d).astype(jnp.float32))


def _mm_kernel(a_ref, w_ref, o_ref, *, relu2):
    acc = jnp.dot(a_ref[...], w_ref[...], preferred_element_type=jnp.float32)
    if relu2:
        acc = jnp.maximum(acc, 0.0)
        acc = acc * acc
    o_ref[...] = acc.astype(o_ref.dtype)


def _mm_res_kernel(a_ref, w_ref, r_ref, o_ref):
    k = pl.program_id(2)
    acc = jnp.dot(a_ref[...], w_ref[...], preferred_element_type=jnp.float32)

    @pl.when(k == 0)
    def _():
        o_ref[...] = r_ref[...] + acc

    @pl.when(k != 0)
    def _():
        o_ref[...] += acc


def _matmul(a, w, out_dtype, relu2=False):
    m, kd = a.shape
    n = w.shape[1]
    tm, tn = _tile(m, 1024), _tile(n, 1024)
    return pl.pallas_call(
        functools.partial(_mm_kernel, relu2=relu2),
        grid=(m // tm, n // tn),
        in_specs=[pl.BlockSpec((tm, kd), lambda i, j: (i, 0)),
                  pl.BlockSpec((kd, tn), lambda i, j: (0, j))],
        out_specs=pl.BlockSpec((tm, tn), lambda i, j: (i, j)),
        out_shape=jax.ShapeDtypeStruct((m, n), out_dtype),
        compiler_params=_params("parallel", "parallel"),
        name="matmul_relu2" if relu2 else "matmul",
    )(a, w)


def _matmul_residual(a, w, res):
    m, kd = a.shape
    n = w.shape[1]
    tm, tn, tk = _tile(m, 1024), _tile(n, 1024), _tile(kd, 4096)
    return pl.pallas_call(
        _mm_res_kernel,
        grid=(m // tm, n // tn, kd // tk),
        in_specs=[pl.BlockSpec((tm, tk), lambda i, j, k: (i, k)),
                  pl.BlockSpec((tk, tn), lambda i, j, k: (k, j)),
                  pl.BlockSpec((tm, tn), lambda i, j, k: (i, j), pipeline_mode=pl.Buffered(1))],
        out_specs=pl.BlockSpec((tm, tn), lambda i, j, k: (i, j)),
        out_shape=jax.ShapeDtypeStruct((m, n), jnp.float32),
        input_output_aliases={2: 0},
        compiler_params=_params("parallel", "parallel", "arbitrary"),
        name="matmul_residual",
    )(a, w, res)


def _na_bias_table(rpb, scale):
    h = rpb.shape[0]
    w = GRID_W
    col = np.arange(w)
    cs = np.clip(col - NA_KW // 2, 0, w - NA_KW)
    dcol = col[None, :] - col[:, None] + (NA_KW - 1)
    col_ok = (col[None, :] >= cs[:, None]) & (col[None, :] < cs[:, None] + NA_KW)
    per_row = jnp.where(jnp.asarray(col_ok)[None, None],
                        rpb.astype(jnp.float32)[:, :, np.clip(dcol, 0, 2 * NA_KW - 2)] / scale,
                        NEG)
    ro = np.arange(NA_ROWS)[None, :] - np.arange(NA_ROWS)[:, None] + (NA_ROWS - 1)
    tab = per_row[:, ro]
    return tab.transpose(0, 1, 3, 2, 4).reshape(h, NA_ROWS, w, NA_ROWS * w)


def _na_kernel(q_ref, k_ref, v_ref, bias_ref, *rest, rows, scale):
    o_ref = rest[-1]
    w = GRID_W

    def body(i, carry):
        staged = []
        for u in range(NA_UNROLL):
            r = i * NA_UNROLL + u
            rs = jnp.clip(r - NA_ROWS // 2, 0, rows - NA_ROWS)
            qt = pl.multiple_of(r * w, w)
            kt = pl.multiple_of(rs * w, w)
            s = lax.dot_general(q_ref[pl.ds(qt, w), :], k_ref[pl.ds(kt, NA_ROWS * w), :], _NT,
                                preferred_element_type=jnp.float32)
            staged.append((s, r - rs, qt, kt))
        probs = []
        for s, i_row, qt, kt in staged:
            t = s + bias_ref[i_row]
            e = jnp.exp2((t - jnp.max(t, axis=-1, keepdims=True)) * (scale * LOG2E))
            probs.append((e.astype(jnp.bfloat16), jnp.sum(e, axis=-1, keepdims=True), qt, kt))
        for e, l, qt, kt in probs:
            o = jnp.dot(e, v_ref[pl.ds(kt, NA_ROWS * w), :], preferred_element_type=jnp.float32)
            o_ref[pl.ds(qt, w), :] = (o / l).astype(o_ref.dtype)
        return carry

    lax.fori_loop(0, rows // NA_UNROLL, body, 0)


def _na_attention(qkv, bias_tab, out, tok0, n_seq, t):
    m = qkv.shape[0]
    d = qkv.shape[1] // 3
    nh = d // HEAD
    rows = t // GRID_W
    assert t % GRID_W == 0 and rows % NA_UNROLL == 0 and rows >= NA_ROWS and tok0 % t == 0
    s0 = tok0 // t
    scale = HEAD ** -0.5
    in_specs = [pl.BlockSpec((t, HEAD), lambda s, h: (s0 + s, h)),
                pl.BlockSpec((t, HEAD), lambda s, h: (s0 + s, nh + h)),
                pl.BlockSpec((t, HEAD), lambda s, h: (s0 + s, 2 * nh + h)),
                pl.BlockSpec((None, NA_ROWS, GRID_W, NA_ROWS * GRID_W), lambda s, h: (h, 0, 0, 0))]
    args = [qkv, qkv, qkv, bias_tab]
    aliases = {}
    if out is not None:
        in_specs.append(pl.BlockSpec(memory_space=pl.ANY))
        args.append(out)
        aliases = {4: 0}
    return pl.pallas_call(
        functools.partial(_na_kernel, rows=rows, scale=scale),
        grid=(n_seq, nh),
        in_specs=in_specs,
        out_specs=pl.BlockSpec((t, HEAD), lambda s, h: (s0 + s, h)),
        out_shape=jax.ShapeDtypeStruct((m, d), jnp.bfloat16),
        input_output_aliases=aliases,
        compiler_params=_params("parallel", "parallel"),
        name="na_attention",
    )(*args)


def _split3(x):
    hi = x.astype(jnp.bfloat16)
    r1 = x - hi.astype(jnp.float32)
    mid = r1.astype(jnp.bfloat16)
    lo = (r1 - mid.astype(jnp.float32)).astype(jnp.bfloat16)
    return hi, mid, lo


def _scan_kernel(q_ref, v_ref, z_ref, loglb_ref, log1m_ref, onem_ref, *rest, reverse, final):
    if final:
        g_ref, of_ref, gain_ref = rest[:3]
    o_ref, b_ref, key_ref, st_ref, lev_ref, sgn_ref = rest[-6:]
    c = SCAN_CHUNK
    ng = c // SUBLANES
    step = pl.program_id(2)
    qbit = 0 if reverse else 1

    @pl.when(step == 0)
    def _():
        st_ref[...] = jnp.zeros_like(st_ref)
        ti = lax.broadcasted_iota(jnp.int32, (c, c), 0)
        si = lax.broadcasted_iota(jnp.int32, (c, c), 1)
        x = ti ^ si
        lev = jnp.zeros((c, c), jnp.int32)
        for l in range(1, SCAN_LEVELS):
            lev = lev + jnp.where(x >= (1 << l), 1, 0)
        lev_ref[...] = jnp.where((si > ti) if reverse else (si < ti), lev, -1)
        row = lax.broadcasted_iota(jnp.int32, (c, HEAD), 0)
        for l in range(SMALL_LEVELS):
            sgn_ref[l] = jnp.where(((row >> l) & 1) == qbit, LOG2E, -LOG2E)

    z = z_ref[...]
    e = jnp.exp(-jnp.abs(z))
    log_sig = jnp.minimum(z, 0.0) - jnp.log(1.0 + e)
    a = loglb_ref[...]
    bb = log1m_ref[...] + log_sig
    log_f = jnp.maximum(a, bb) + jnp.log(1.0 + jnp.exp(-jnp.abs(a - bb)))
    key = onem_ref[...] * (jnp.where(z >= 0.0, e, 1.0) / (1.0 + e))

    ti = lax.broadcasted_iota(jnp.int32, (c, c), 0)
    si = lax.broadcasted_iota(jnp.int32, (c, c), 1)
    tri = jnp.where((si >= ti) if reverse else (si <= ti), 1.0, 0.0).astype(jnp.bfloat16)
    hi, mid, lo = _split3(log_f)
    b_all = (jnp.dot(tri, hi, preferred_element_type=jnp.float32)
             + jnp.dot(tri, mid, preferred_element_type=jnp.float32)
             + jnp.dot(tri, lo, preferred_element_type=jnp.float32))
    b_ref[...] = b_all
    key_ref[...] = key

    sub = lax.broadcasted_iota(jnp.int32, (SUBLANES, HEAD), 0)
    last = 0 if reverse else c - 1

    def brow(r, cols, n=SUBLANES):
        return jnp.broadcast_to(b_ref[r:r + 1, cols], (n, HEAD))

    def boundary(blk, half):
        return blk + (half if reverse else half - 1)

    def finish(h, o):
        cols = slice(h * HEAD, (h + 1) * HEAD)
        if final:
            o = o + of_ref[:, cols]
            o = o * lax.rsqrt(jnp.mean(o * o, axis=-1, keepdims=True) + RMS_EPS) * gain_ref[...]
            gt = g_ref[:, cols]
            o = o * (gt / (1.0 + jnp.exp(-gt)))
        o_ref[:, cols] = o.astype(o_ref.dtype)

    def heads_direct():
        staged = []
        for h in range(SCAN_HEADS):
            cols = slice(h * HEAD, (h + 1) * HEAD)
            vb = v_ref[:, cols].astype(jnp.bfloat16)
            b = b_ref[:, cols]
            qs = (q_ref[:, cols] * jnp.exp(b)).astype(jnp.bfloat16)
            kx = key_ref[:, cols] * jnp.exp(-b)
            attn = lax.dot_general(qs, kx.astype(jnp.bfloat16), _NT, preferred_element_type=jnp.float32)
            st = st_ref[h]
            inter = lax.dot_general(qs, st.astype(jnp.bfloat16), _NT, preferred_element_type=jnp.float32)
            e_last = jnp.exp(b_ref[last:last + 1, cols])
            st_ref[h] = st * e_last + lax.dot_general(
                vb, (kx * e_last).astype(jnp.bfloat16), _TN, preferred_element_type=jnp.float32)
            staged.append((attn, vb, inter))
        for h, (attn, vb, inter) in enumerate(staged):
            attn = attn.astype(jnp.bfloat16) * tri
            finish(h, jnp.dot(attn, vb, preferred_element_type=jnp.float32) + inter)

    def head_levels(h):
        cols = slice(h * HEAD, (h + 1) * HEAD)
        q = q_ref[:, cols]
        v = v_ref[:, cols]
        kk = key_ref[:, cols]
        b = b_ref[:, cols]
        attn = [jnp.zeros((SUBLANES, c), jnp.float32) for _ in range(ng)]

        for l in range(SMALL_LEVELS):
            half = 1 << l
            n = SUBLANES >> (l + 1)
            pieces = []
            for g in range(ng):
                cand = [brow(boundary(g * SUBLANES + (j << (l + 1)), half), cols) for j in range(n)]
                p = cand[-1]
                for j in range(n - 2, -1, -1):
                    p = jnp.where(sub < ((j + 1) << (l + 1)), cand[j], p)
                pieces.append(p)
            sgn = sgn_ref[l]
            fac = jnp.exp2((b - jnp.concatenate(pieces, axis=0)) * sgn)
            xl = (jnp.where(sgn > 0.0, q, kk) * fac).astype(jnp.bfloat16)
            zl = lax.dot_general(xl, xl, _NT, preferred_element_type=jnp.float32)
            for g in range(ng):
                rows = slice(g * SUBLANES, (g + 1) * SUBLANES)
                attn[g] = jnp.where(lev_ref[rows, :] == l, zl[rows], attn[g])

        for l in range(SMALL_LEVELS, SCAN_LEVELS):
            half = 1 << l
            xs, xq, q_groups = [], [], []
            for part in range(c // half):
                rows = slice(part * half, (part + 1) * half)
                bound = brow(boundary((part >> 1) * 2 * half, half), cols, half)
                if (part & 1) == qbit:
                    xp = (q[rows] * jnp.exp(b[rows] - bound)).astype(jnp.bfloat16)
                    xq.append(xp)
                    q_groups += range(part * half // SUBLANES, (part + 1) * half // SUBLANES)
                else:
                    xp = (kk[rows] * jnp.exp(bound - b[rows])).astype(jnp.bfloat16)
                xs.append(xp)
            zl = lax.dot_general(jnp.concatenate(xq, axis=0), jnp.concatenate(xs, axis=0), _NT,
                                 preferred_element_type=jnp.float32)
            for i, g in enumerate(q_groups):
                rows = slice(g * SUBLANES, (g + 1) * SUBLANES)
                attn[g] = jnp.where(lev_ref[rows, :] == l, zl[i * SUBLANES:(i + 1) * SUBLANES], attn[g])

        attn = jnp.concatenate(attn, axis=0).astype(jnp.bfloat16)
        o = jnp.dot(attn, v.astype(jnp.bfloat16), preferred_element_type=jnp.float32)
        o = o + jnp.sum(q * kk, axis=-1, keepdims=True) * v
        st = st_ref[h]
        qs = (q * jnp.exp(b)).astype(jnp.bfloat16)
        o = o + lax.dot_general(qs, st.astype(jnp.bfloat16), _NT, preferred_element_type=jnp.float32)
        b_last = b_ref[last:last + 1, cols]
        kh = (kk * jnp.exp(b_last - b)).astype(jnp.bfloat16)
        st_ref[h] = st * jnp.exp(b_last) + lax.dot_general(
            v.astype(jnp.bfloat16), kh, _TN, preferred_element_type=jnp.float32)
        finish(h, o)

    total = jnp.min(b_ref[last:last + 1, :], axis=1, keepdims=True)[0, 0]

    @pl.when(total >= -SAFE_DECAY)
    def _():
        heads_direct()

    @pl.when(jnp.logical_not(total >= -SAFE_DECAY))
    def _():
        for h in range(SCAN_HEADS):
            head_levels(h)


def _scan(proj, lbs, o_fwd, gain, out, tok0, n_seq, t):
    m = proj.shape[0]
    d = proj.shape[1] // 5
    c, gw = SCAN_CHUNK, SCAN_HEADS * HEAD
    assert t % c == 0 and tok0 % c == 0 and d % gw == 0
    nc, ng, cb0 = t // c, d // gw, tok0 // c
    final = o_fwd is not None

    def blk(which):
        return pl.BlockSpec((c, gw), lambda s, g, ci: (cb0 + s * nc + (nc - 1 - ci if final else ci), which * ng + g))

    vec = pl.BlockSpec((1, gw), lambda s, g, ci: (0, g))
    in_specs = [blk(0), blk(1), blk(3 if final else 2), vec, vec, vec]
    args = [proj, proj, proj, *lbs]
    if final:
        in_specs += [blk(4), blk(0), pl.BlockSpec((1, HEAD), lambda s, g, ci: (0, 0))]
        args += [proj, o_fwd, gain]
    aliases = {}
    if out is not None:
        aliases = {len(args): 0}
        in_specs.append(pl.BlockSpec(memory_space=pl.ANY))
        args.append(out)
    return pl.pallas_call(
        functools.partial(_scan_kernel, reverse=final, final=final),
        grid=(n_seq, ng, nc),
        in_specs=in_specs,
        out_specs=blk(0),
        out_shape=jax.ShapeDtypeStruct((m, d), jnp.bfloat16 if final else jnp.float32),
        scratch_shapes=[pltpu.VMEM((c, gw), jnp.float32),
                        pltpu.VMEM((c, gw), jnp.float32),
                        pltpu.VMEM((SCAN_HEADS, HEAD, HEAD), jnp.float32),
                        pltpu.VMEM((c, c), jnp.int32),
                        pltpu.VMEM((SMALL_LEVELS, c, HEAD), jnp.float32)],
        input_output_aliases=aliases,
        compiler_params=_params("parallel", "parallel", "arbitrary"),
        name="hgrn_scan_bwd" if final else "hgrn_scan_fwd",
    )(*args)


def kernel(x_prompt, x_sample, ln_mix, ln_mlp, ln_final, w_qkv, rpb, w_o_a, w_in_b, lower_bounds, g_norm,
           w_o_b, w_up, w_down):
    d = x_prompt.shape[-1]
    depth = ln_mix.shape[0]
    groups = []
    tok = 0
    for a in (x_prompt, x_sample):
        groups.append((tok, a.shape[0], a.shape[1]))
        tok += a.shape[0] * a.shape[1]
    x = jnp.concatenate([x_prompt.reshape(-1, d), x_sample.reshape(-1, d)], axis=0)

    bf = lambda w: w.astype(jnp.bfloat16)
    loglb, log1m, onem = _lb_schedule(lower_bounds)

    for layer in range(depth):
        j = layer // 2
        h = _rmsnorm(x, ln_mix[layer], jnp.bfloat16)
        if layer % 2 == 0:
            qkv = _matmul(h, bf(w_qkv[j]), jnp.bfloat16)
            bias_tab = _na_bias_table(rpb[j], HEAD ** -0.5)
            o = None
            for g in groups:
                o = _na_attention(qkv, bias_tab, o, *g)
            x = _matmul_residual(o, bf(w_o_a[j]), x)
        else:
            proj = _matmul(h, bf(w_in_b[j]), jnp.float32)
            lbs = [a[layer].reshape(1, d) for a in (loglb, log1m, onem)]
            gain = g_norm[j].reshape(1, HEAD).astype(jnp.float32)
            o_fwd = None
            for g in groups:
                o_fwd = _scan(proj, lbs, None, None, o_fwd, *g)
            o = None
            for g in groups:
                o = _scan(proj, lbs, o_fwd, gain, o, *g)
            x = _matmul_residual(o, bf(w_o_b[j]), x)
        u = _matmul(_rmsnorm(x, ln_mlp[layer], jnp.bfloat16), bf(w_up[layer]), jnp.bfloat16, relu2=True)
        x = _matmul_residual(u, bf(w_down[layer]), x)

    outs = []
    for (tok0, n_seq, t), a in zip(groups, (x_prompt, x_sample)):
        y = _rmsnorm(x, ln_final, jnp.float32, row0=tok0, rows=n_seq * t)
        outs.append(y.reshape(a.shape))
    return tuple(outs)
```

```python
import functools
import math

import numpy as np
import jax
import jax.numpy as jnp
from jax import lax
from jax.experimental import pallas as pl
from jax.experimental.pallas import tpu as pltpu

HEAD = 128
SUBLANES = 8
GRID_W = 64
NA_ROWS = 8
NA_KW = 16
NA_UNROLL = 16
SCAN_CHUNK = 128
SCAN_LEVELS = 7
SMALL_LEVELS = 3
SCAN_HEADS = 8
SAFE_DECAY = 60.0
RMS_EPS = 1e-6
NEG = -1e30
LOG2E = math.log2(math.e)
VMEM_LIMIT = 56 * 1024 * 1024

_NT = (((1,), (1,)), ((), ()))
_TN = (((0,), (0,)), ((), ()))


def _params(*sem):
    return pltpu.CompilerParams(dimension_semantics=sem, vmem_limit_bytes=VMEM_LIMIT)


def _tile(n, pref):
    t = min(n, pref)
    while n % t:
        t //= 2
    return t


def _lb_kernel(x_ref, loglb_ref, log1m_ref, onem_ref):
    x = x_ref[...]
    e = jnp.exp(x - jnp.max(x, axis=0, keepdims=True))
    sm = e / jnp.sum(e, axis=0, keepdims=True)
    depth = x.shape[0]
    c = sm[0:1]
    c0 = c
    for r in range(depth):
        if r:
            c = c + sm[r:r + 1]
        lb = c - c0
        loglb_ref[r:r + 1, :] = jnp.log(lb)
        log1m_ref[r:r + 1, :] = jnp.log1p(-lb)
        onem_ref[r:r + 1, :] = 1.0 - lb


def _lb_schedule(lower_bounds):
    depth, d = lower_bounds.shape
    out = jax.ShapeDtypeStruct((depth, d), jnp.float32)
    return pl.pallas_call(_lb_kernel, out_shape=(out, out, out), name="lb_schedule")(
        lower_bounds.astype(jnp.float32))


def _rmsnorm_kernel(x_ref, g_ref, o_ref):
    x = x_ref[...]
    y = x * lax.rsqrt(jnp.mean(x * x, axis=-1, keepdims=True) + RMS_EPS)
    o_ref[...] = (y * g_ref[...]).astype(o_ref.dtype)


def _rmsnorm(x, g, out_dtype, row0=0, rows=None):
    m, d = x.shape
    rows = m if rows is None else rows
    tm = _tile(rows, 256)
    assert row0 % tm == 0
    off = row0 // tm
    return pl.pallas_call(
        _rmsnorm_kernel,
        grid=(rows // tm,),
        in_specs=[pl.BlockSpec((tm, d), lambda i: (i + off, 0)),
                  pl.BlockSpec((1, d), lambda i: (0, 0))],
        out_specs=pl.BlockSpec((tm, d), lambda i: (i, 0)),
        out_shape=jax.ShapeDtypeStruct((rows, d), out_dtype),
        compiler_params=_params("parallel"),
        name="rmsnorm",
    )(x, g.reshape(1, d).astype(jnp.float32))


def _mm_kernel(a_ref, w_ref, o_ref, *, relu2):
    acc = jnp.dot(a_ref[...], w_ref[...], preferred_element_type=jnp.float32)
    if relu2:
        acc = jnp.maximum(acc, 0.0)
        acc = acc * acc
    o_ref[...] = acc.astype(o_ref.dtype)


def _mm_res_kernel(a_ref, w_ref, r_ref, o_ref, *, k_steps):
    if k_steps == 1:
        o_ref[...] = r_ref[...] + jnp.dot(a_ref[...], w_ref[...], preferred_element_type=jnp.float32)
    else:
        @pl.when(pl.program_id(2) == 0)
        def _():
            o_ref[...] = r_ref[...]

        o_ref[...] += jnp.dot(a_ref[...], w_ref[...], preferred_element_type=jnp.float32)


def _cast_kernel(x_ref, o_ref):
    o_ref[...] = x_ref[...].astype(o_ref.dtype)


def _to_bf16(w):
    l, kd, n = w.shape
    bk, bn = _tile(kd, 512), _tile(n, 4096)
    spec = pl.BlockSpec((None, bk, bn), lambda a, i, j: (a, i, j))
    return pl.pallas_call(
        _cast_kernel,
        grid=(l, kd // bk, n // bn),
        in_specs=[spec],
        out_specs=spec,
        out_shape=jax.ShapeDtypeStruct(w.shape, jnp.bfloat16),
        compiler_params=_params("parallel", "parallel", "parallel"),
        name="to_bf16",
    )(w)


def _matmul(a, w, layer, out_dtype, relu2=False):
    m, kd = a.shape
    n = w.shape[2]
    tm, tn = _tile(m, 1024), _tile(n, 1024)
    return pl.pallas_call(
        functools.partial(_mm_kernel, relu2=relu2),
        grid=(m // tm, n // tn),
        in_specs=[pl.BlockSpec((tm, kd), lambda i, j: (i, 0)),
                  pl.BlockSpec((None, kd, tn), lambda i, j: (layer, 0, j))],
        out_specs=pl.BlockSpec((tm, tn), lambda i, j: (i, j)),
        out_shape=jax.ShapeDtypeStruct((m, n), out_dtype),
        compiler_params=_params("parallel", "parallel"),
        name="matmul_relu2" if relu2 else "matmul",
    )(a, w)


def _matmul_residual(a, w, layer, res, tm, tn, tk):
    m, kd = a.shape
    n = w.shape[2]
    tm, tn, tk = _tile(m, tm), _tile(n, tn), _tile(kd, tk)
    return pl.pallas_call(
        functools.partial(_mm_res_kernel, k_steps=kd // tk),
        grid=(m // tm, n // tn, kd // tk),
        in_specs=[pl.BlockSpec((tm, tk), lambda i, j, k: (i, k)),
                  pl.BlockSpec((None, tk, tn), lambda i, j, k: (layer, k, j)),
                  pl.BlockSpec((tm, tn), lambda i, j, k: (i, j))],
        out_specs=pl.BlockSpec((tm, tn), lambda i, j, k: (i, j)),
        out_shape=jax.ShapeDtypeStruct((m, n), jnp.float32),
        input_output_aliases={2: 0},
        compiler_params=_params("parallel", "parallel", "arbitrary"),
        name="matmul_residual",
    )(a, w, res)


def _na_bias_table(rpb, scale):
    h = rpb.shape[0]
    w = GRID_W
    col = np.arange(w)
    cs = np.clip(col - NA_KW // 2, 0, w - NA_KW)
    dcol = col[None, :] - col[:, None] + (NA_KW - 1)
    col_ok = (col[None, :] >= cs[:, None]) & (col[None, :] < cs[:, None] + NA_KW)
    per_row = jnp.where(jnp.asarray(col_ok)[None, None],
                        rpb.astype(jnp.float32)[:, :, np.clip(dcol, 0, 2 * NA_KW - 2)] / scale,
                        NEG)
    ro = np.arange(NA_ROWS)[None, :] - np.arange(NA_ROWS)[:, None] + (NA_ROWS - 1)
    tab = per_row[:, ro]
    return tab.transpose(0, 1, 3, 2, 4).reshape(h, NA_ROWS, w, NA_ROWS * w)


def _na_kernel(q_ref, k_ref, v_ref, bias_ref, *rest, rows, scale):
    o_ref = rest[-1]
    w = GRID_W

    def body(i, carry):
        staged = []
        for u in range(NA_UNROLL):
            r = i * NA_UNROLL + u
            rs = jnp.clip(r - NA_ROWS // 2, 0, rows - NA_ROWS)
            qt = pl.multiple_of(r * w, w)
            kt = pl.multiple_of(rs * w, w)
            s = lax.dot_general(q_ref[pl.ds(qt, w), :], k_ref[pl.ds(kt, NA_ROWS * w), :], _NT,
                                preferred_element_type=jnp.float32)
            staged.append((s, r - rs, qt, kt))
        probs = []
        for s, i_row, qt, kt in staged:
            t = s + bias_ref[i_row]
            e = jnp.exp2((t - jnp.max(t, axis=-1, keepdims=True)) * (scale * LOG2E))
            probs.append((e.astype(jnp.bfloat16), jnp.sum(e, axis=-1, keepdims=True), qt, kt))
        for e, l, qt, kt in probs:
            o = jnp.dot(e, v_ref[pl.ds(kt, NA_ROWS * w), :], preferred_element_type=jnp.float32)
            o_ref[pl.ds(qt, w), :] = (o / l).astype(o_ref.dtype)
        return carry

    lax.fori_loop(0, rows // NA_UNROLL, body, 0)


def _na_attention(qkv, bias_tab, out, tok0, n_seq, t):
    m = qkv.shape[0]
    d = qkv.shape[1] // 3
    nh = d // HEAD
    rows = t // GRID_W
    assert t % GRID_W == 0 and rows % NA_UNROLL == 0 and rows >= NA_ROWS and tok0 % t == 0
    s0 = tok0 // t
    scale = HEAD ** -0.5
    in_specs = [pl.BlockSpec((t, HEAD), lambda s, h: (s0 + s, h)),
                pl.BlockSpec((t, HEAD), lambda s, h: (s0 + s, nh + h)),
                pl.BlockSpec((t, HEAD), lambda s, h: (s0 + s, 2 * nh + h)),
                pl.BlockSpec((None, NA_ROWS, GRID_W, NA_ROWS * GRID_W), lambda s, h: (h, 0, 0, 0))]
    args = [qkv, qkv, qkv, bias_tab]
    aliases = {}
    if out is not None:
        in_specs.append(pl.BlockSpec(memory_space=pl.ANY))
        args.append(out)
        aliases = {4: 0}
    return pl.pallas_call(
        functools.partial(_na_kernel, rows=rows, scale=scale),
        grid=(n_seq, nh),
        in_specs=in_specs,
        out_specs=pl.BlockSpec((t, HEAD), lambda s, h: (s0 + s, h)),
        out_shape=jax.ShapeDtypeStruct((m, d), jnp.bfloat16),
        input_output_aliases=aliases,
        compiler_params=_params("parallel", "parallel"),
        name="na_attention",
    )(*args)


def _split3(x):
    hi = x.astype(jnp.bfloat16)
    r1 = x - hi.astype(jnp.float32)
    mid = r1.astype(jnp.bfloat16)
    lo = (r1 - mid.astype(jnp.float32)).astype(jnp.bfloat16)
    return hi, mid, lo


def _scan_kernel(q_ref, v_ref, z_ref, loglb_ref, log1m_ref, onem_ref, *rest, reverse, final):
    if final:
        g_ref, of_ref, gain_ref = rest[:3]
    o_ref, b_ref, key_ref, st_ref, lev_ref, sgn_ref, tri_ref, worst_ref = rest[-8:]
    c = SCAN_CHUNK
    ng = c // SUBLANES
    step = pl.program_id(2)
    qbit = 0 if reverse else 1
    last = 0 if reverse else c - 1
    mid = c // 2 if reverse else c // 2 - 1

    def gates(slot):
        z = z_ref[...]
        e = jnp.exp(-jnp.abs(z))
        log_sig = jnp.minimum(z, 0.0) * LOG2E - jnp.log2(1.0 + e)
        a = loglb_ref[...] * LOG2E
        bb = log1m_ref[...] * LOG2E + log_sig
        log_f = jnp.maximum(a, bb) + jnp.log2(1.0 + jnp.exp2(-jnp.abs(a - bb)))
        key_ref[slot] = onem_ref[...] * (jnp.where(z >= 0.0, e, 1.0) / (1.0 + e))
        tri = tri_ref[...]
        hi, md, lo = _split3(log_f)
        b_all = (jnp.dot(tri, hi, preferred_element_type=jnp.float32)
                 + jnp.dot(tri, md, preferred_element_type=jnp.float32)
                 + jnp.dot(tri, lo, preferred_element_type=jnp.float32))
        b_ref[slot] = b_all
        b_mid = b_all[mid:mid + 1, :]
        half = jnp.minimum(b_mid, b_all[last:last + 1, :] - b_mid)
        worst_ref[slot] = jnp.min(half, axis=1, keepdims=True)[0, 0]

    @pl.when(step == 0)
    def _():
        st_ref[...] = jnp.zeros_like(st_ref)
        ti = lax.broadcasted_iota(jnp.int32, (c, c), 0)
        si = lax.broadcasted_iota(jnp.int32, (c, c), 1)
        tri_ref[...] = jnp.where((si >= ti) if reverse else (si <= ti), 1.0, 0.0).astype(jnp.bfloat16)
        x = ti ^ si
        lev = jnp.zeros((c, c), jnp.int32)
        for l in range(1, SCAN_LEVELS):
            lev = lev + jnp.where(x >= (1 << l), 1, 0)
        lev_ref[...] = jnp.where((si > ti) if reverse else (si < ti), lev, -1)
        row = lax.broadcasted_iota(jnp.int32, (c, HEAD), 0)
        for l in range(SMALL_LEVELS):
            sgn_ref[l] = jnp.where(((row >> l) & 1) == qbit, 1.0, -1.0)
        gates(0)

    sub = lax.broadcasted_iota(jnp.int32, (SUBLANES, HEAD), 0)
    cur = (step + 1) % 2
    nxt = step % 2

    def brow(r, cols, n=SUBLANES):
        return jnp.broadcast_to(b_ref[cur, r:r + 1, cols], (n, HEAD))

    def boundary(blk, half):
        return blk + (half if reverse else half - 1)

    def finish(h, o):
        cols = slice(h * HEAD, (h + 1) * HEAD)
        if final:
            o = o + of_ref[:, cols]
            o = o * lax.rsqrt(jnp.mean(o * o, axis=-1, keepdims=True) + RMS_EPS) * gain_ref[...]
            gt = g_ref[:, cols]
            o = o * (gt / (1.0 + jnp.exp(-gt)))
        o_ref[:, cols] = o.astype(o_ref.dtype)

    def heads_direct():
        tri = tri_ref[...]
        staged = []
        for h in range(SCAN_HEADS):
            cols = slice(h * HEAD, (h + 1) * HEAD)
            vb = v_ref[:, cols].astype(jnp.bfloat16)
            b_mid = b_ref[cur, mid:mid + 1, cols]
            b_last = b_ref[cur, last:last + 1, cols]
            d = b_ref[cur, :, cols] - b_mid
            qm = q_ref[:, cols] * jnp.exp2(d)
            km = key_ref[cur, :, cols] * jnp.exp2(-d)
            attn = lax.dot_general(qm.astype(jnp.bfloat16), km.astype(jnp.bfloat16), _NT,
                                   preferred_element_type=jnp.float32)
            st = st_ref[h]
            inter = lax.dot_general((qm * jnp.exp2(b_mid)).astype(jnp.bfloat16), st.astype(jnp.bfloat16), _NT,
                                    preferred_element_type=jnp.float32)
            st_ref[h] = st * jnp.exp2(b_last) + lax.dot_general(
                vb, (km * jnp.exp2(b_last - b_mid)).astype(jnp.bfloat16), _TN, preferred_element_type=jnp.float32)
            staged.append((attn, vb, inter))
        gates(nxt)
        keep = tri > 0
        for h, (attn, vb, inter) in enumerate(staged):
            attn = jnp.where(keep, attn.astype(jnp.bfloat16), 0)
            finish(h, jnp.dot(attn, vb, preferred_element_type=jnp.float32) + inter)

    def head_levels(h):
        cols = slice(h * HEAD, (h + 1) * HEAD)
        q = q_ref[:, cols]
        v = v_ref[:, cols]
        kk = key_ref[cur, :, cols]
        b = b_ref[cur, :, cols]
        attn = [jnp.zeros((SUBLANES, c), jnp.float32) for _ in range(ng)]

        for l in range(SMALL_LEVELS):
            half = 1 << l
            n = SUBLANES >> (l + 1)
            pieces = []
            for g in range(ng):
                cand = [brow(boundary(g * SUBLANES + (j << (l + 1)), half), cols) for j in range(n)]
                p = cand[-1]
                for j in range(n - 2, -1, -1):
                    p = jnp.where(sub < ((j + 1) << (l + 1)), cand[j], p)
                pieces.append(p)
            sgn = sgn_ref[l]
            fac = jnp.exp2((b - jnp.concatenate(pieces, axis=0)) * sgn)
            xl = (jnp.where(sgn > 0.0, q, kk) * fac).astype(jnp.bfloat16)
            zl = lax.dot_general(xl, xl, _NT, preferred_element_type=jnp.float32)
            for g in range(ng):
                rows = slice(g * SUBLANES, (g + 1) * SUBLANES)
                attn[g] = jnp.where(lev_ref[rows, :] == l, zl[rows], attn[g])

        for l in range(SMALL_LEVELS, SCAN_LEVELS):
            half = 1 << l
            xs, xq, q_groups = [], [], []
            for part in range(c // half):
                rows = slice(part * half, (part + 1) * half)
                bound = brow(boundary((part >> 1) * 2 * half, half), cols, half)
                if (part & 1) == qbit:
                    xp = (q[rows] * jnp.exp2(b[rows] - bound)).astype(jnp.bfloat16)
                    xq.append(xp)
                    q_groups += range(part * half // SUBLANES, (part + 1) * half // SUBLANES)
                else:
                    xp = (kk[rows] * jnp.exp2(bound - b[rows])).astype(jnp.bfloat16)
                xs.append(xp)
            zl = lax.dot_general(jnp.concatenate(xq, axis=0), jnp.concatenate(xs, axis=0), _NT,
                                 preferred_element_type=jnp.float32)
            for i, g in enumerate(q_groups):
                rows = slice(g * SUBLANES, (g + 1) * SUBLANES)
                attn[g] = jnp.where(lev_ref[rows, :] == l, zl[i * SUBLANES:(i + 1) * SUBLANES], attn[g])

        attn = jnp.concatenate(attn, axis=0).astype(jnp.bfloat16)
        o = jnp.dot(attn, v.astype(jnp.bfloat16), preferred_element_type=jnp.float32)
        o = o + jnp.sum(q * kk, axis=-1, keepdims=True) * v
        st = st_ref[h]
        qs = (q * jnp.exp2(b)).astype(jnp.bfloat16)
        o = o + lax.dot_general(qs, st.astype(jnp.bfloat16), _NT, preferred_element_type=jnp.float32)
        b_last = b_ref[cur, last:last + 1, cols]
        kh = (kk * jnp.exp2(b_last - b)).astype(jnp.bfloat16)
        st_ref[h] = st * jnp.exp2(b_last) + lax.dot_general(
            v.astype(jnp.bfloat16), kh, _TN, preferred_element_type=jnp.float32)
        finish(h, o)

    @pl.when(step > 0)
    def _():
        direct_ok = worst_ref[cur] >= -SAFE_DECAY * LOG2E

        @pl.when(direct_ok)
        def _():
            heads_direct()

        @pl.when(jnp.logical_not(direct_ok))
        def _():
            gates(nxt)
            for h in range(SCAN_HEADS):
                head_levels(h)


def _scan(proj, lbs, o_fwd, gain, out, tok0, n_seq, t):
    m = proj.shape[0]
    d = proj.shape[1] // 5
    c, gw = SCAN_CHUNK, SCAN_HEADS * HEAD
    assert t % c == 0 and tok0 % c == 0 and d % gw == 0
    nc, ng, cb0 = t // c, d // gw, tok0 // c
    final = o_fwd is not None

    def blk(which, lag=1):
        def index(s, g, i):
            ci = jnp.clip(i - lag, 0, nc - 1)
            return cb0 + s * nc + (nc - 1 - ci if final else ci), which * ng + g
        return pl.BlockSpec((c, gw), index)

    vec = pl.BlockSpec((1, gw), lambda s, g, i: (0, g))
    in_specs = [blk(0), blk(1), blk(3 if final else 2, lag=0), vec, vec, vec]
    args = [proj, proj, proj, *lbs]
    if final:
        in_specs += [blk(4), blk(0), pl.BlockSpec((1, HEAD), lambda s, g, ci: (0, 0))]
        args += [proj, o_fwd, gain]
    aliases = {}
    if out is not None:
        aliases = {len(args): 0}
        in_specs.append(pl.BlockSpec(memory_space=pl.ANY))
        args.append(out)
    return pl.pallas_call(
        functools.partial(_scan_kernel, reverse=final, final=final),
        grid=(n_seq, ng, nc + 1),
        in_specs=in_specs,
        out_specs=blk(0),
        out_shape=jax.ShapeDtypeStruct((m, d), jnp.bfloat16 if final else jnp.float32),
        scratch_shapes=[pltpu.VMEM((2, c, gw), jnp.float32),
                        pltpu.VMEM((2, c, gw), jnp.float32),
                        pltpu.VMEM((SCAN_HEADS, HEAD, HEAD), jnp.float32),
                        pltpu.VMEM((c, c), jnp.int32),
                        pltpu.VMEM((SMALL_LEVELS, c, HEAD), jnp.float32),
                        pltpu.VMEM((c, c), jnp.bfloat16),
                        pltpu.SMEM((2,), jnp.float32)],
        input_output_aliases=aliases,
        compiler_params=_params("parallel", "parallel", "arbitrary"),
        name="hgrn_scan_bwd" if final else "hgrn_scan_fwd",
    )(*args)


def kernel(x_prompt, x_sample, ln_mix, ln_mlp, ln_final, w_qkv, rpb, w_o_a, w_in_b, lower_bounds, g_norm,
           w_o_b, w_up, w_down):
    d = x_prompt.shape[-1]
    depth = ln_mix.shape[0]
    groups = []
    tok = 0
    for a in (x_prompt, x_sample):
        groups.append((tok, a.shape[0], a.shape[1]))
        tok += a.shape[0] * a.shape[1]
    x = jnp.concatenate([x_prompt.reshape(-1, d), x_sample.reshape(-1, d)], axis=0)

    w_qkv, w_o_a, w_in_b, w_o_b, w_up, w_down = (_to_bf16(w) for w in (w_qkv, w_o_a, w_in_b, w_o_b, w_up, w_down))
    loglb, log1m, onem = _lb_schedule(lower_bounds)
    proj_tiles = (1024, 512, 4096)
    down_tiles = (2048, 1024, 1024)

    for layer in range(depth):
        j = layer // 2
        h = _rmsnorm(x, ln_mix[layer], jnp.bfloat16)
        if layer % 2 == 0:
            qkv = _matmul(h, w_qkv, j, jnp.bfloat16)
            bias_tab = _na_bias_table(rpb[j], HEAD ** -0.5)
            o = None
            for g in groups:
                o = _na_attention(qkv, bias_tab, o, *g)
            x = _matmul_residual(o, w_o_a, j, x, *proj_tiles)
        else:
            proj = _matmul(h, w_in_b, j, jnp.float32)
            lbs = [a[layer].reshape(1, d) for a in (loglb, log1m, onem)]
            gain = g_norm[j].reshape(1, HEAD).astype(jnp.float32)
            o_fwd = None
            for g in groups:
                o_fwd = _scan(proj, lbs, None, None, o_fwd, *g)
            o = None
            for g in groups:
                o = _scan(proj, lbs, o_fwd, gain, o, *g)
            x = _matmul_residual(o, w_o_b, j, x, *proj_tiles)
        u = _matmul(_rmsnorm(x, ln_mlp[layer], jnp.bfloat16), w_up, layer, jnp.bfloat16, relu2=True)
        x = _matmul_residual(u, w_down, layer, x, *down_tiles)

    outs = []
    for (tok0, n_seq, t), a in zip(groups, (x_prompt, x_sample)):
        y = _rmsnorm(x, ln_final, jnp.float32, row0=tok0, rows=n_seq * t)
        outs.append(y.reshape(a.shape))
    return tuple(outs)
```

```python
import functools
import math

import numpy as np
import jax
import jax.numpy as jnp
from jax import lax
from jax.experimental import pallas as pl
from jax.experimental.pallas import tpu as pltpu

HEAD = 128
SUBLANES = 8
GRID_W = 64
NA_ROWS = 8
NA_KW = 16
NA_UNROLL = 16
SCAN_CHUNK = 128
SCAN_LEVELS = 7
SMALL_LEVELS = 3
SCAN_HEADS = 16
SAFE_DECAY = 60.0
RMS_EPS = 1e-6
NEG = -1e30
LOG2E = math.log2(math.e)
VMEM_LIMIT = 56 * 1024 * 1024

_NT = (((1,), (1,)), ((), ()))
_TN = (((0,), (0,)), ((), ()))


def _params(*sem):
    return pltpu.CompilerParams(dimension_semantics=sem, vmem_limit_bytes=VMEM_LIMIT)


def _tile(n, pref):
    if pref <= n and n % pref == 0:
        return pref
    t = 1
    while t * 2 <= min(n, pref) and n % (t * 2) == 0:
        t *= 2
    return t


def _lb_kernel(x_ref, loglb_ref, log1m_ref, onem_ref):
    x = x_ref[...]
    e = jnp.exp(x - jnp.max(x, axis=0, keepdims=True))
    sm = e / jnp.sum(e, axis=0, keepdims=True)
    depth = x.shape[0]
    c = sm[0:1]
    c0 = c
    for r in range(depth):
        if r:
            c = c + sm[r:r + 1]
        lb = c - c0
        loglb_ref[r:r + 1, :] = jnp.log(lb)
        log1m_ref[r:r + 1, :] = jnp.log1p(-lb)
        onem_ref[r:r + 1, :] = 1.0 - lb


def _lb_schedule(lower_bounds):
    depth, d = lower_bounds.shape
    out = jax.ShapeDtypeStruct((depth, d), jnp.float32)
    return pl.pallas_call(_lb_kernel, out_shape=(out, out, out), name="lb_schedule")(
        lower_bounds.astype(jnp.float32))


def _rmsnorm_kernel(x_ref, g_ref, o_ref):
    x = x_ref[...]
    y = x * lax.rsqrt(jnp.mean(x * x, axis=-1, keepdims=True) + RMS_EPS)
    o_ref[...] = (y * g_ref[...]).astype(o_ref.dtype)


def _rmsnorm(x, g, out_dtype, row0=0, rows=None):
    m, d = x.shape
    rows = m if rows is None else rows
    tm = _tile(rows, 256)
    assert row0 % tm == 0
    off = row0 // tm
    return pl.pallas_call(
        _rmsnorm_kernel,
        grid=(rows // tm,),
        in_specs=[pl.BlockSpec((tm, d), lambda i: (i + off, 0)),
                  pl.BlockSpec((1, d), lambda i: (0, 0))],
        out_specs=pl.BlockSpec((tm, d), lambda i: (i, 0)),
        out_shape=jax.ShapeDtypeStruct((rows, d), out_dtype),
        compiler_params=_params("parallel"),
        name="rmsnorm",
    )(x, g.reshape(1, d).astype(jnp.float32))


def _mm_kernel(a_ref, w_ref, *rest, relu2, inv_d):
    o_ref = rest[-1]
    acc = jnp.dot(a_ref[...], w_ref[...], preferred_element_type=jnp.float32)
    if inv_d is not None:
        acc = acc * lax.rsqrt(rest[0][:, 0:1] * inv_d + RMS_EPS)
    if relu2:
        acc = jnp.maximum(acc, 0.0)
        acc = acc * acc
    o_ref[...] = acc.astype(o_ref.dtype)


def _mm_res_norm_kernel(a_ref, w_ref, r_ref, g_ref, o_ref, hb_ref, ssq_ref):
    @pl.when(pl.program_id(1) == 0)
    def _():
        ssq_ref[...] = jnp.zeros_like(ssq_ref)

    x = r_ref[...] + jnp.dot(a_ref[...], w_ref[...], preferred_element_type=jnp.float32)
    o_ref[...] = x
    hb_ref[...] = (x * g_ref[...]).astype(hb_ref.dtype)
    ssq_ref[...] += jnp.sum(x * x, axis=-1, keepdims=True)


def _mm_res_kernel(a_ref, w_ref, r_ref, o_ref, *, k_steps):
    if k_steps == 1:
        o_ref[...] = r_ref[...] + jnp.dot(a_ref[...], w_ref[...], preferred_element_type=jnp.float32)
    else:
        @pl.when(pl.program_id(2) == 0)
        def _():
            o_ref[...] = r_ref[...]

        o_ref[...] += jnp.dot(a_ref[...], w_ref[...], preferred_element_type=jnp.float32)


def _cast_kernel(x_ref, o_ref):
    o_ref[...] = x_ref[...].astype(o_ref.dtype)


def _to_bf16(w):
    l, kd, n = w.shape
    bk, bn = _tile(kd, 512), _tile(n, 4096)
    spec = pl.BlockSpec((None, bk, bn), lambda a, i, j: (a, i, j))
    return pl.pallas_call(
        _cast_kernel,
        grid=(l, kd // bk, n // bn),
        in_specs=[spec],
        out_specs=spec,
        out_shape=jax.ShapeDtypeStruct(w.shape, jnp.bfloat16),
        compiler_params=_params("parallel", "parallel", "parallel"),
        name="to_bf16",
    )(w)


def _matmul(a, w, layer, out_dtype, relu2=False, ssq=None):
    m, kd = a.shape
    n = w.shape[2]
    tm, tn = _tile(m, 1024), _tile(n, 1024)
    in_specs = [pl.BlockSpec((tm, kd), lambda i, j: (i, 0)),
                pl.BlockSpec((None, kd, tn), lambda i, j: (layer, 0, j))]
    args = [a, w]
    if ssq is not None:
        in_specs.append(pl.BlockSpec((tm, HEAD), lambda i, j: (i, 0)))
        args.append(ssq)
    return pl.pallas_call(
        functools.partial(_mm_kernel, relu2=relu2, inv_d=None if ssq is None else 1.0 / kd),
        grid=(m // tm, n // tn),
        in_specs=in_specs,
        out_specs=pl.BlockSpec((tm, tn), lambda i, j: (i, j)),
        out_shape=jax.ShapeDtypeStruct((m, n), out_dtype),
        compiler_params=_params("parallel", "parallel"),
        name="matmul_relu2" if relu2 else "matmul",
    )(*args)


def _matmul_residual_norm(a, w, layer, res, gain, tm, tn):
    m, kd = a.shape
    n = w.shape[2]
    tm, tn = _tile(m, tm), _tile(n, tn)
    tile = pl.BlockSpec((tm, tn), lambda i, j: (i, j))
    return pl.pallas_call(
        _mm_res_norm_kernel,
        grid=(m // tm, n // tn),
        in_specs=[pl.BlockSpec((tm, kd), lambda i, j: (i, 0)),
                  pl.BlockSpec((None, kd, tn), lambda i, j: (layer, 0, j)),
                  tile,
                  pl.BlockSpec((1, tn), lambda i, j: (0, j))],
        out_specs=[tile, tile, pl.BlockSpec((tm, HEAD), lambda i, j: (i, 0))],
        out_shape=[jax.ShapeDtypeStruct((m, n), jnp.float32),
                   jax.ShapeDtypeStruct((m, n), jnp.bfloat16),
                   jax.ShapeDtypeStruct((m, HEAD), jnp.float32)],
        input_output_aliases={2: 0},
        compiler_params=_params("parallel", "arbitrary"),
        name="matmul_residual_norm",
    )(a, w, res, gain.reshape(1, n).astype(jnp.float32))


def _matmul_residual(a, w, layer, res, tm, tn, tk):
    m, kd = a.shape
    n = w.shape[2]
    tm, tn, tk = _tile(m, tm), _tile(n, tn), _tile(kd, tk)
    return pl.pallas_call(
        functools.partial(_mm_res_kernel, k_steps=kd // tk),
        grid=(m // tm, n // tn, kd // tk),
        in_specs=[pl.BlockSpec((tm, tk), lambda i, j, k: (i, k)),
                  pl.BlockSpec((None, tk, tn), lambda i, j, k: (layer, k, j)),
                  pl.BlockSpec((tm, tn), lambda i, j, k: (i, j))],
        out_specs=pl.BlockSpec((tm, tn), lambda i, j, k: (i, j)),
        out_shape=jax.ShapeDtypeStruct((m, n), jnp.float32),
        input_output_aliases={2: 0},
        compiler_params=_params("parallel", "parallel", "arbitrary"),
        name="matmul_residual",
    )(a, w, res)


def _na_bias_table(rpb, scale):
    h = rpb.shape[0]
    w = GRID_W
    col = np.arange(w)
    cs = np.clip(col - NA_KW // 2, 0, w - NA_KW)
    dcol = col[None, :] - col[:, None] + (NA_KW - 1)
    col_ok = (col[None, :] >= cs[:, None]) & (col[None, :] < cs[:, None] + NA_KW)
    per_row = jnp.where(jnp.asarray(col_ok)[None, None],
                        rpb.astype(jnp.float32)[:, :, np.clip(dcol, 0, 2 * NA_KW - 2)] / scale,
                        NEG)
    ro = np.arange(NA_ROWS)[None, :] - np.arange(NA_ROWS)[:, None] + (NA_ROWS - 1)
    tab = per_row[:, ro]
    return tab.transpose(0, 1, 3, 2, 4).reshape(h, NA_ROWS, w, NA_ROWS * w)


def _na_kernel(q_ref, k_ref, v_ref, bias_ref, *rest, rows, scale):
    o_ref = rest[-1]
    w = GRID_W

    def body(i, carry):
        staged = []
        for u in range(NA_UNROLL):
            r = i * NA_UNROLL + u
            rs = jnp.clip(r - NA_ROWS // 2, 0, rows - NA_ROWS)
            qt = pl.multiple_of(r * w, w)
            kt = pl.multiple_of(rs * w, w)
            s = lax.dot_general(q_ref[pl.ds(qt, w), :], k_ref[pl.ds(kt, NA_ROWS * w), :], _NT,
                                preferred_element_type=jnp.float32)
            staged.append((s, r - rs, qt, kt))
        probs = []
        for s, i_row, qt, kt in staged:
            t = s + bias_ref[i_row]
            e = jnp.exp2((t - jnp.max(t, axis=-1, keepdims=True)) * (scale * LOG2E))
            probs.append((e.astype(jnp.bfloat16), jnp.sum(e, axis=-1, keepdims=True), qt, kt))
        for e, l, qt, kt in probs:
            o = jnp.dot(e, v_ref[pl.ds(kt, NA_ROWS * w), :], preferred_element_type=jnp.float32)
            o_ref[pl.ds(qt, w), :] = (o / l).astype(o_ref.dtype)
        return carry

    lax.fori_loop(0, rows // NA_UNROLL, body, 0)


def _na_attention(qkv, bias_tab, out, tok0, n_seq, t):
    m = qkv.shape[0]
    d = qkv.shape[1] // 3
    nh = d // HEAD
    rows = t // GRID_W
    assert t % GRID_W == 0 and rows % NA_UNROLL == 0 and rows >= NA_ROWS and tok0 % t == 0
    s0 = tok0 // t
    scale = HEAD ** -0.5
    in_specs = [pl.BlockSpec((t, HEAD), lambda s, h: (s0 + s, h)),
                pl.BlockSpec((t, HEAD), lambda s, h: (s0 + s, nh + h)),
                pl.BlockSpec((t, HEAD), lambda s, h: (s0 + s, 2 * nh + h)),
                pl.BlockSpec((None, NA_ROWS, GRID_W, NA_ROWS * GRID_W), lambda s, h: (h, 0, 0, 0))]
    args = [qkv, qkv, qkv, bias_tab]
    aliases = {}
    if out is not None:
        in_specs.append(pl.BlockSpec(memory_space=pl.ANY))
        args.append(out)
        aliases = {4: 0}
    return pl.pallas_call(
        functools.partial(_na_kernel, rows=rows, scale=scale),
        grid=(n_seq, nh),
        in_specs=in_specs,
        out_specs=pl.BlockSpec((t, HEAD), lambda s, h: (s0 + s, h)),
        out_shape=jax.ShapeDtypeStruct((m, d), jnp.bfloat16),
        input_output_aliases=aliases,
        compiler_params=_params("parallel", "parallel"),
        name="na_attention",
    )(*args)


def _split3(x):
    hi = x.astype(jnp.bfloat16)
    r1 = x - hi.astype(jnp.float32)
    mid = r1.astype(jnp.bfloat16)
    lo = (r1 - mid.astype(jnp.float32)).astype(jnp.bfloat16)
    return hi, mid, lo


def _scan_kernel(q_ref, v_ref, z_ref, loglb_ref, log1m_ref, onem_ref, *rest, reverse, final):
    if final:
        g_ref, of_ref, gain_ref = rest[:3]
    o_ref, b_ref, key_ref, st_ref, lev_ref, sgn_ref, tri_ref, worst_ref = rest[-8:]
    c = SCAN_CHUNK
    ng = c // SUBLANES
    step = pl.program_id(2)
    qbit = 0 if reverse else 1
    last = 0 if reverse else c - 1
    mid = c // 2 if reverse else c // 2 - 1

    def gates(slot):
        z = z_ref[...]
        e = jnp.exp(-jnp.abs(z))
        log_sig = jnp.minimum(z, 0.0) * LOG2E - jnp.log2(1.0 + e)
        a = loglb_ref[...] * LOG2E
        bb = log1m_ref[...] * LOG2E + log_sig
        log_f = jnp.maximum(a, bb) + jnp.log2(1.0 + jnp.exp2(-jnp.abs(a - bb)))
        key_ref[slot] = onem_ref[...] * (jnp.where(z >= 0.0, e, 1.0) / (1.0 + e))
        tri = tri_ref[...]
        hi, md, lo = _split3(log_f)
        b_all = (jnp.dot(tri, hi, preferred_element_type=jnp.float32)
                 + jnp.dot(tri, md, preferred_element_type=jnp.float32)
                 + jnp.dot(tri, lo, preferred_element_type=jnp.float32))
        b_ref[slot] = b_all
        b_mid = b_all[mid:mid + 1, :]
        half = jnp.minimum(b_mid, b_all[last:last + 1, :] - b_mid)
        worst_ref[slot] = jnp.min(half, axis=1, keepdims=True)[0, 0]

    @pl.when(step == 0)
    def _():
        st_ref[...] = jnp.zeros_like(st_ref)
        ti = lax.broadcasted_iota(jnp.int32, (c, c), 0)
        si = lax.broadcasted_iota(jnp.int32, (c, c), 1)
        tri_ref[...] = jnp.where((si >= ti) if reverse else (si <= ti), 1.0, 0.0).astype(jnp.bfloat16)
        x = ti ^ si
        lev = jnp.zeros((c, c), jnp.int32)
        for l in range(1, SCAN_LEVELS):
            lev = lev + jnp.where(x >= (1 << l), 1, 0)
        lev_ref[...] = jnp.where((si > ti) if reverse else (si < ti), lev, -1)
        row = lax.broadcasted_iota(jnp.int32, (c, HEAD), 0)
        for l in range(SMALL_LEVELS):
            sgn_ref[l] = jnp.where(((row >> l) & 1) == qbit, 1.0, -1.0)
        gates(0)

    sub = lax.broadcasted_iota(jnp.int32, (SUBLANES, HEAD), 0)
    cur = (step + 1) % 2
    nxt = step % 2

    def brow(r, cols, n=SUBLANES):
        return jnp.broadcast_to(b_ref[cur, r:r + 1, cols], (n, HEAD))

    def boundary(blk, half):
        return blk + (half if reverse else half - 1)

    def finish(h, o):
        cols = slice(h * HEAD, (h + 1) * HEAD)
        if final:
            o = o + of_ref[:, cols]
            o = o * lax.rsqrt(jnp.mean(o * o, axis=-1, keepdims=True) + RMS_EPS) * gain_ref[...]
            gt = g_ref[:, cols]
            o = o * (gt / (1.0 + jnp.exp(-gt)))
        o_ref[:, cols] = o.astype(o_ref.dtype)

    def heads_direct():
        tri = tri_ref[...]
        staged = []
        for h in range(SCAN_HEADS):
            cols = slice(h * HEAD, (h + 1) * HEAD)
            vb = v_ref[:, cols].astype(jnp.bfloat16)
            b_mid = b_ref[cur, mid:mid + 1, cols]
            b_last = b_ref[cur, last:last + 1, cols]
            d = b_ref[cur, :, cols] - b_mid
            qm = q_ref[:, cols] * jnp.exp2(d)
            km = key_ref[cur, :, cols] * jnp.exp2(-d)
            attn = lax.dot_general(qm.astype(jnp.bfloat16), km.astype(jnp.bfloat16), _NT,
                                   preferred_element_type=jnp.float32)
            st = st_ref[h]
            inter = lax.dot_general((qm * jnp.exp2(b_mid)).astype(jnp.bfloat16), st.astype(jnp.bfloat16), _NT,
                                    preferred_element_type=jnp.float32)
            st_ref[h] = st * jnp.exp2(b_last) + lax.dot_general(
                vb, (km * jnp.exp2(b_last - b_mid)).astype(jnp.bfloat16), _TN, preferred_element_type=jnp.float32)
            staged.append((attn, vb, inter))
        gates(nxt)
        keep = tri > 0
        for h, (attn, vb, inter) in enumerate(staged):
            attn = jnp.where(keep, attn.astype(jnp.bfloat16), 0)
            finish(h, jnp.dot(attn, vb, preferred_element_type=jnp.float32) + inter)

    def head_levels(h):
        cols = slice(h * HEAD, (h + 1) * HEAD)
        q = q_ref[:, cols]
        v = v_ref[:, cols]
        kk = key_ref[cur, :, cols]
        b = b_ref[cur, :, cols]
        attn = [jnp.zeros((SUBLANES, c), jnp.float32) for _ in range(ng)]

        for l in range(SMALL_LEVELS):
            half = 1 << l
            n = SUBLANES >> (l + 1)
            pieces = []
            for g in range(ng):
                cand = [brow(boundary(g * SUBLANES + (j << (l + 1)), half), cols) for j in range(n)]
                p = cand[-1]
                for j in range(n - 2, -1, -1):
                    p = jnp.where(sub < ((j + 1) << (l + 1)), cand[j], p)
                pieces.append(p)
            sgn = sgn_ref[l]
            fac = jnp.exp2((b - jnp.concatenate(pieces, axis=0)) * sgn)
            xl = (jnp.where(sgn > 0.0, q, kk) * fac).astype(jnp.bfloat16)
            zl = lax.dot_general(xl, xl, _NT, preferred_element_type=jnp.float32)
            for g in range(ng):
                rows = slice(g * SUBLANES, (g + 1) * SUBLANES)
                attn[g] = jnp.where(lev_ref[rows, :] == l, zl[rows], attn[g])

        for l in range(SMALL_LEVELS, SCAN_LEVELS):
            half = 1 << l
            xs, xq, q_groups = [], [], []
            for part in range(c // half):
                rows = slice(part * half, (part + 1) * half)
                bound = brow(boundary((part >> 1) * 2 * half, half), cols, half)
                if (part & 1) == qbit:
                    xp = (q[rows] * jnp.exp2(b[rows] - bound)).astype(jnp.bfloat16)
                    xq.append(xp)
                    q_groups += range(part * half // SUBLANES, (part + 1) * half // SUBLANES)
                else:
                    xp = (kk[rows] * jnp.exp2(bound - b[rows])).astype(jnp.bfloat16)
                xs.append(xp)
            zl = lax.dot_general(jnp.concatenate(xq, axis=0), jnp.concatenate(xs, axis=0), _NT,
                                 preferred_element_type=jnp.float32)
            for i, g in enumerate(q_groups):
                rows = slice(g * SUBLANES, (g + 1) * SUBLANES)
                attn[g] = jnp.where(lev_ref[rows, :] == l, zl[i * SUBLANES:(i + 1) * SUBLANES], attn[g])

        attn = jnp.concatenate(attn, axis=0).astype(jnp.bfloat16)
        o = jnp.dot(attn, v.astype(jnp.bfloat16), preferred_element_type=jnp.float32)
        o = o + jnp.sum(q * kk, axis=-1, keepdims=True) * v
        st = st_ref[h]
        qs = (q * jnp.exp2(b)).astype(jnp.bfloat16)
        o = o + lax.dot_general(qs, st.astype(jnp.bfloat16), _NT, preferred_element_type=jnp.float32)
        b_last = b_ref[cur, last:last + 1, cols]
        kh = (kk * jnp.exp2(b_last - b)).astype(jnp.bfloat16)
        st_ref[h] = st * jnp.exp2(b_last) + lax.dot_general(
            v.astype(jnp.bfloat16), kh, _TN, preferred_element_type=jnp.float32)
        finish(h, o)

    @pl.when(step > 0)
    def _():
        direct_ok = worst_ref[cur] >= -SAFE_DECAY * LOG2E

        @pl.when(direct_ok)
        def _():
            heads_direct()

        @pl.when(jnp.logical_not(direct_ok))
        def _():
            gates(nxt)
            for h in range(SCAN_HEADS):
                head_levels(h)


def _scan(proj, lbs, o_fwd, gain, out, tok0, n_seq, t):
    m = proj.shape[0]
    d = proj.shape[1] // 5
    c, gw = SCAN_CHUNK, SCAN_HEADS * HEAD
    assert t % c == 0 and tok0 % c == 0 and d % gw == 0
    nc, ng, cb0 = t // c, d // gw, tok0 // c
    final = o_fwd is not None

    def blk(which, lag=1):
        def index(s, g, i):
            ci = jnp.clip(i - lag, 0, nc - 1)
            return cb0 + s * nc + (nc - 1 - ci if final else ci), which * ng + g
        return pl.BlockSpec((c, gw), index)

    vec = pl.BlockSpec((1, gw), lambda s, g, i: (0, g))
    in_specs = [blk(0), blk(1), blk(3 if final else 2, lag=0), vec, vec, vec]
    args = [proj, proj, proj, *lbs]
    if final:
        in_specs += [blk(4), blk(0), pl.BlockSpec((1, HEAD), lambda s, g, ci: (0, 0))]
        args += [proj, o_fwd, gain]
    aliases = {}
    if out is not None:
        aliases = {len(args): 0}
        in_specs.append(pl.BlockSpec(memory_space=pl.ANY))
        args.append(out)
    return pl.pallas_call(
        functools.partial(_scan_kernel, reverse=final, final=final),
        grid=(n_seq, ng, nc + 1),
        in_specs=in_specs,
        out_specs=blk(0),
        out_shape=jax.ShapeDtypeStruct((m, d), jnp.bfloat16 if final else jnp.float32),
        scratch_shapes=[pltpu.VMEM((2, c, gw), jnp.float32),
                        pltpu.VMEM((2, c, gw), jnp.float32),
                        pltpu.VMEM((SCAN_HEADS, HEAD, HEAD), jnp.float32),
                        pltpu.VMEM((c, c), jnp.int32),
                        pltpu.VMEM((SMALL_LEVELS, c, HEAD), jnp.float32),
                        pltpu.VMEM((c, c), jnp.bfloat16),
                        pltpu.SMEM((2,), jnp.float32)],
        input_output_aliases=aliases,
        compiler_params=_params("parallel", "parallel", "arbitrary"),
        name="hgrn_scan_bwd" if final else "hgrn_scan_fwd",
    )(*args)


def kernel(x_prompt, x_sample, ln_mix, ln_mlp, ln_final, w_qkv, rpb, w_o_a, w_in_b, lower_bounds, g_norm,
           w_o_b, w_up, w_down):
    d = x_prompt.shape[-1]
    depth = ln_mix.shape[0]
    groups = []
    tok = 0
    for a in (x_prompt, x_sample):
        groups.append((tok, a.shape[0], a.shape[1]))
        tok += a.shape[0] * a.shape[1]
    x = jnp.concatenate([x_prompt.reshape(-1, d), x_sample.reshape(-1, d)], axis=0)

    w_qkv, w_o_a, w_in_b, w_o_b, w_up, w_down = (_to_bf16(w) for w in (w_qkv, w_o_a, w_in_b, w_o_b, w_up, w_down))
    loglb, log1m, onem = _lb_schedule(lower_bounds)
    proj_tiles = (1024, 512)
    down_tiles = (2048, 1024, 1024)

    for layer in range(depth):
        j = layer // 2
        h = _rmsnorm(x, ln_mix[layer], jnp.bfloat16)
        if layer % 2 == 0:
            qkv = _matmul(h, w_qkv, j, jnp.bfloat16)
            bias_tab = _na_bias_table(rpb[j], HEAD ** -0.5)
            o = None
            for g in groups:
                o = _na_attention(qkv, bias_tab, o, *g)
            x, hb, ssq = _matmul_residual_norm(o, w_o_a, j, x, ln_mlp[layer], *proj_tiles)
        else:
            proj = _matmul(h, w_in_b, j, jnp.float32)
            lbs = [a[layer].reshape(1, d) for a in (loglb, log1m, onem)]
            gain = g_norm[j].reshape(1, HEAD).astype(jnp.float32)
            o_fwd = None
            for g in groups:
                o_fwd = _scan(proj, lbs, None, None, o_fwd, *g)
            o = None
            for g in groups:
                o = _scan(proj, lbs, o_fwd, gain, o, *g)
            x, hb, ssq = _matmul_residual_norm(o, w_o_b, j, x, ln_mlp[layer], *proj_tiles)
        u = _matmul(hb, w_up, layer, jnp.bfloat16, relu2=True, ssq=ssq)
        x = _matmul_residual(u, w_down, layer, x, *down_tiles)

    outs = []
    for (tok0, n_seq, t), a in zip(groups, (x_prompt, x_sample)):
        y = _rmsnorm(x, ln_final, jnp.float32, row0=tok0, rows=n_seq * t)
        outs.append(y.reshape(a.shape))
    return tuple(outs)
```

```python
import functools
import math

import numpy as np
import jax
import jax.numpy as jnp
from jax import lax
from jax.experimental import pallas as pl
from jax.experimental.pallas import tpu as pltpu

HEAD = 128
SUBLANES = 8
GRID_W = 64
NA_ROWS = 8
NA_KW = 16
NA_UNROLL = 32
SCAN_CHUNK = 128
SCAN_LEVELS = 7
SMALL_LEVELS = 3
SCAN_HEADS = 16
SAFE_DECAY = 60.0
RMS_EPS = 1e-6
NEG = -1e30
LOG2E = math.log2(math.e)
VMEM_LIMIT = 56 * 1024 * 1024

_NT = (((1,), (1,)), ((), ()))
_TN = (((0,), (0,)), ((), ()))


def _params(*sem):
    return pltpu.CompilerParams(dimension_semantics=sem, vmem_limit_bytes=VMEM_LIMIT)


def _tile(n, pref):
    if pref <= n and n % pref == 0:
        return pref
    t = 1
    while t * 2 <= min(n, pref) and n % (t * 2) == 0:
        t *= 2
    return t


def _lb_kernel(x_ref, lb_ref, onem_ref):
    x = x_ref[...]
    e = jnp.exp(x - jnp.max(x, axis=0, keepdims=True))
    sm = e / jnp.sum(e, axis=0, keepdims=True)
    depth = x.shape[0]
    c = sm[0:1]
    c0 = c
    for r in range(depth):
        if r:
            c = c + sm[r:r + 1]
        lb = c - c0
        lb_ref[r:r + 1, :] = lb
        onem_ref[r:r + 1, :] = 1.0 - lb


def _lb_schedule(lower_bounds):
    depth, d = lower_bounds.shape
    out = jax.ShapeDtypeStruct((depth, d), jnp.float32)
    return pl.pallas_call(_lb_kernel, out_shape=(out, out), name="lb_schedule")(
        lower_bounds.astype(jnp.float32))


def _rmsnorm_kernel(x_ref, g_ref, o_ref):
    x = x_ref[...]
    y = x * lax.rsqrt(jnp.mean(x * x, axis=-1, keepdims=True) + RMS_EPS)
    o_ref[...] = (y * g_ref[...]).astype(o_ref.dtype)


def _rmsnorm(x, g, out_dtype, row0=0, rows=None):
    m, d = x.shape
    rows = m if rows is None else rows
    tm = _tile(rows, 512)
    assert row0 % tm == 0
    off = row0 // tm
    return pl.pallas_call(
        _rmsnorm_kernel,
        grid=(rows // tm,),
        in_specs=[pl.BlockSpec((tm, d), lambda i: (i + off, 0)),
                  pl.BlockSpec((1, d), lambda i: (0, 0))],
        out_specs=pl.BlockSpec((tm, d), lambda i: (i, 0)),
        out_shape=jax.ShapeDtypeStruct((rows, d), out_dtype),
        compiler_params=_params("parallel"),
        name="rmsnorm",
    )(x, g.reshape(1, d).astype(jnp.float32))


def _mm_kernel(a_ref, w_ref, *rest, relu2, inv_d):
    o_ref = rest[-1]
    acc = jnp.dot(a_ref[...], w_ref[...], preferred_element_type=jnp.float32)
    if inv_d is not None:
        acc = acc * lax.rsqrt(rest[0][:, 0:1] * inv_d + RMS_EPS)
    if relu2:
        acc = jnp.maximum(acc, 0.0)
        acc = acc * acc
    o_ref[...] = acc.astype(o_ref.dtype)


def _mm_res_norm_kernel(a_ref, w_ref, r_ref, g_ref, o_ref, hb_ref, ssq_ref):
    @pl.when(pl.program_id(1) == 0)
    def _():
        ssq_ref[...] = jnp.zeros_like(ssq_ref)

    x = r_ref[...] + jnp.dot(a_ref[...], w_ref[...], preferred_element_type=jnp.float32)
    o_ref[...] = x
    hb_ref[...] = (x * g_ref[...]).astype(hb_ref.dtype)
    ssq_ref[...] += jnp.sum(x * x, axis=-1, keepdims=True)


def _mm_res_kernel(a_ref, w_ref, r_ref, o_ref, *, k_steps):
    if k_steps == 1:
        o_ref[...] = r_ref[...] + jnp.dot(a_ref[...], w_ref[...], preferred_element_type=jnp.float32)
    else:
        @pl.when(pl.program_id(2) == 0)
        def _():
            o_ref[...] = r_ref[...]

        o_ref[...] += jnp.dot(a_ref[...], w_ref[...], preferred_element_type=jnp.float32)


def _cast_kernel(x_ref, o_ref):
    o_ref[...] = x_ref[...].astype(o_ref.dtype)


def _to_bf16(w):
    l, kd, n = w.shape
    bk, bn = _tile(kd, 512), _tile(n, 4096)
    spec = pl.BlockSpec((None, bk, bn), lambda a, i, j: (a, i, j))
    return pl.pallas_call(
        _cast_kernel,
        grid=(l, kd // bk, n // bn),
        in_specs=[spec],
        out_specs=spec,
        out_shape=jax.ShapeDtypeStruct(w.shape, jnp.bfloat16),
        compiler_params=_params("parallel", "parallel", "parallel"),
        name="to_bf16",
    )(w)


def _matmul(a, w, layer, out_dtype, relu2=False, ssq=None):
    m, kd = a.shape
    n = w.shape[2]
    tm, tn = _tile(m, 1024), _tile(n, 1024)
    in_specs = [pl.BlockSpec((tm, kd), lambda i, j: (i, 0)),
                pl.BlockSpec((None, kd, tn), lambda i, j: (layer, 0, j))]
    args = [a, w]
    if ssq is not None:
        in_specs.append(pl.BlockSpec((tm, HEAD), lambda i, j: (i, 0)))
        args.append(ssq)
    return pl.pallas_call(
        functools.partial(_mm_kernel, relu2=relu2, inv_d=None if ssq is None else 1.0 / kd),
        grid=(m // tm, n // tn),
        in_specs=in_specs,
        out_specs=pl.BlockSpec((tm, tn), lambda i, j: (i, j)),
        out_shape=jax.ShapeDtypeStruct((m, n), out_dtype),
        compiler_params=_params("parallel", "parallel"),
        name="matmul_relu2" if relu2 else "matmul",
    )(*args)


def _matmul_residual_norm(a, w, layer, res, gain, tm, tn):
    m, kd = a.shape
    n = w.shape[2]
    tm, tn = _tile(m, tm), _tile(n, tn)
    tile = pl.BlockSpec((tm, tn), lambda i, j: (i, j))
    return pl.pallas_call(
        _mm_res_norm_kernel,
        grid=(m // tm, n // tn),
        in_specs=[pl.BlockSpec((tm, kd), lambda i, j: (i, 0)),
                  pl.BlockSpec((None, kd, tn), lambda i, j: (layer, 0, j)),
                  tile,
                  pl.BlockSpec((1, tn), lambda i, j: (0, j))],
        out_specs=[tile, tile, pl.BlockSpec((tm, HEAD), lambda i, j: (i, 0))],
        out_shape=[jax.ShapeDtypeStruct((m, n), jnp.float32),
                   jax.ShapeDtypeStruct((m, n), jnp.bfloat16),
                   jax.ShapeDtypeStruct((m, HEAD), jnp.float32)],
        input_output_aliases={2: 0},
        compiler_params=_params("parallel", "arbitrary"),
        name="matmul_residual_norm",
    )(a, w, res, gain.reshape(1, n).astype(jnp.float32))


def _matmul_residual(a, w, layer, res, tm, tn, tk):
    m, kd = a.shape
    n = w.shape[2]
    tm, tn, tk = _tile(m, tm), _tile(n, tn), _tile(kd, tk)
    return pl.pallas_call(
        functools.partial(_mm_res_kernel, k_steps=kd // tk),
        grid=(m // tm, n // tn, kd // tk),
        in_specs=[pl.BlockSpec((tm, tk), lambda i, j, k: (i, k)),
                  pl.BlockSpec((None, tk, tn), lambda i, j, k: (layer, k, j)),
                  pl.BlockSpec((tm, tn), lambda i, j, k: (i, j))],
        out_specs=pl.BlockSpec((tm, tn), lambda i, j, k: (i, j)),
        out_shape=jax.ShapeDtypeStruct((m, n), jnp.float32),
        input_output_aliases={2: 0},
        compiler_params=_params("parallel", "parallel", "arbitrary"),
        name="matmul_residual",
    )(a, w, res)


def _na_bias_table(rpb, scale):
    h = rpb.shape[0]
    w = GRID_W
    col = np.arange(w)
    cs = np.clip(col - NA_KW // 2, 0, w - NA_KW)
    dcol = col[None, :] - col[:, None] + (NA_KW - 1)
    col_ok = (col[None, :] >= cs[:, None]) & (col[None, :] < cs[:, None] + NA_KW)
    per_row = jnp.where(jnp.asarray(col_ok)[None, None],
                        rpb.astype(jnp.float32)[:, :, np.clip(dcol, 0, 2 * NA_KW - 2)] / scale,
                        NEG)
    ro = np.arange(NA_ROWS)[None, :] - np.arange(NA_ROWS)[:, None] + (NA_ROWS - 1)
    tab = per_row[:, ro]
    return tab.transpose(0, 1, 3, 2, 4).reshape(h, NA_ROWS, w, NA_ROWS * w)


def _na_kernel(q_ref, k_ref, v_ref, bias_ref, *rest, rows, scale):
    o_ref = rest[-1]
    w = GRID_W
    unroll = math.gcd(rows, NA_UNROLL)

    def body(i, carry):
        staged = []
        for u in range(unroll):
            r = i * unroll + u
            rs = jnp.clip(r - NA_ROWS // 2, 0, rows - NA_ROWS)
            qt = pl.multiple_of(r * w, w)
            kt = pl.multiple_of(rs * w, w)
            s = lax.dot_general(q_ref[pl.ds(qt, w), :], k_ref[pl.ds(kt, NA_ROWS * w), :], _NT,
                                preferred_element_type=jnp.float32)
            staged.append((s, r - rs, qt, kt))
        probs = []
        for s, i_row, qt, kt in staged:
            t = s + bias_ref[i_row]
            e = jnp.exp2((t - jnp.max(t, axis=-1, keepdims=True)) * (scale * LOG2E))
            probs.append((e.astype(jnp.bfloat16), jnp.sum(e, axis=-1, keepdims=True), qt, kt))
        for e, l, qt, kt in probs:
            o = jnp.dot(e, v_ref[pl.ds(kt, NA_ROWS * w), :], preferred_element_type=jnp.float32)
            o_ref[pl.ds(qt, w), :] = (o / l).astype(o_ref.dtype)
        return carry

    lax.fori_loop(0, rows // unroll, body, 0)


def _na_attention(qkv, bias_tab, out, tok0, n_seq, t):
    m = qkv.shape[0]
    d = qkv.shape[1] // 3
    nh = d // HEAD
    rows = t // GRID_W
    assert t % GRID_W == 0 and rows >= NA_ROWS and tok0 % t == 0
    s0 = tok0 // t
    scale = HEAD ** -0.5
    in_specs = [pl.BlockSpec((t, HEAD), lambda s, h: (s0 + s, h)),
                pl.BlockSpec((t, HEAD), lambda s, h: (s0 + s, nh + h)),
                pl.BlockSpec((t, HEAD), lambda s, h: (s0 + s, 2 * nh + h)),
                pl.BlockSpec((None, NA_ROWS, GRID_W, NA_ROWS * GRID_W), lambda s, h: (h, 0, 0, 0))]
    args = [qkv, qkv, qkv, bias_tab]
    aliases = {}
    if out is not None:
        in_specs.append(pl.BlockSpec(memory_space=pl.ANY))
        args.append(out)
        aliases = {4: 0}
    return pl.pallas_call(
        functools.partial(_na_kernel, rows=rows, scale=scale),
        grid=(n_seq, nh),
        in_specs=in_specs,
        out_specs=pl.BlockSpec((t, HEAD), lambda s, h: (s0 + s, h)),
        out_shape=jax.ShapeDtypeStruct((m, d), jnp.bfloat16),
        input_output_aliases=aliases,
        compiler_params=_params("parallel", "parallel"),
        name="na_attention",
    )(*args)


def _split3(x):
    hi = x.astype(jnp.bfloat16)
    r1 = x - hi.astype(jnp.float32)
    mid = r1.astype(jnp.bfloat16)
    lo = (r1 - mid.astype(jnp.float32)).astype(jnp.bfloat16)
    return hi, mid, lo


def _scan_kernel(q_ref, v_ref, z_ref, lb_ref, onem_ref, *rest, reverse, final):
    if final:
        g_ref, of_ref, gain_ref = rest[:3]
    o_ref, b_ref, key_ref, st_ref, lev_ref, sgn_ref, tri_ref, worst_ref = rest[-8:]
    c = SCAN_CHUNK
    ng = c // SUBLANES
    step = pl.program_id(2)
    qbit = 0 if reverse else 1
    last = 0 if reverse else c - 1
    mid = c // 2 if reverse else c // 2 - 1

    def gates(slot):
        z = z_ref[...]
        e = jnp.exp(-jnp.abs(z))
        pos = z >= 0.0
        lb = lb_ref[...]
        den = 1.0 + e
        log_f = jnp.log2(jnp.where(pos, 1.0 + lb * e, lb + e)) - jnp.log2(den)
        key_ref[slot] = onem_ref[...] * (jnp.where(pos, e, 1.0) / den)
        tri = tri_ref[...]
        hi, md, lo = _split3(log_f)
        b_all = (jnp.dot(tri, hi, preferred_element_type=jnp.float32)
                 + jnp.dot(tri, md, preferred_element_type=jnp.float32)
                 + jnp.dot(tri, lo, preferred_element_type=jnp.float32))
        b_ref[slot] = b_all
        b_mid = b_all[mid:mid + 1, :]
        half = jnp.minimum(b_mid, b_all[last:last + 1, :] - b_mid)
        worst_ref[slot] = jnp.min(half, axis=1, keepdims=True)[0, 0]

    @pl.when(step == 0)
    def _():
        st_ref[...] = jnp.zeros_like(st_ref)
        ti = lax.broadcasted_iota(jnp.int32, (c, c), 0)
        si = lax.broadcasted_iota(jnp.int32, (c, c), 1)
        tri_ref[...] = jnp.where((si >= ti) if reverse else (si <= ti), 1.0, 0.0).astype(jnp.bfloat16)
        x = ti ^ si
        lev = jnp.zeros((c, c), jnp.int32)
        for l in range(1, SCAN_LEVELS):
            lev = lev + jnp.where(x >= (1 << l), 1, 0)
        lev_ref[...] = jnp.where((si > ti) if reverse else (si < ti), lev, -1)
        row = lax.broadcasted_iota(jnp.int32, (c, HEAD), 0)
        for l in range(SMALL_LEVELS):
            sgn_ref[l] = jnp.where(((row >> l) & 1) == qbit, 1.0, -1.0)
        gates(0)

    sub = lax.broadcasted_iota(jnp.int32, (SUBLANES, HEAD), 0)
    cur = (step + 1) % 2
    nxt = step % 2

    def brow(r, cols, n=SUBLANES):
        return jnp.broadcast_to(b_ref[cur, r:r + 1, cols], (n, HEAD))

    def boundary(blk, half):
        return blk + (half if reverse else half - 1)

    def finish(h, o):
        cols = slice(h * HEAD, (h + 1) * HEAD)
        if final:
            o = o + of_ref[:, cols]
            o = o * lax.rsqrt(jnp.mean(o * o, axis=-1, keepdims=True) + RMS_EPS) * gain_ref[...]
            gt = g_ref[:, cols]
            o = o * (gt / (1.0 + jnp.exp(-gt)))
        o_ref[:, cols] = o.astype(o_ref.dtype)

    def heads_direct():
        tri = tri_ref[...]
        staged = []
        for h in range(SCAN_HEADS):
            cols = slice(h * HEAD, (h + 1) * HEAD)
            vb = v_ref[:, cols].astype(jnp.bfloat16)
            b_mid = b_ref[cur, mid:mid + 1, cols]
            b_last = b_ref[cur, last:last + 1, cols]
            d = b_ref[cur, :, cols] - b_mid
            qm = q_ref[:, cols] * jnp.exp2(d)
            km = key_ref[cur, :, cols] * jnp.exp2(-d)
            attn = lax.dot_general(qm.astype(jnp.bfloat16), km.astype(jnp.bfloat16), _NT,
                                   preferred_element_type=jnp.float32)
            st = st_ref[h]
            inter = lax.dot_general((qm * jnp.exp2(b_mid)).astype(jnp.bfloat16), st.astype(jnp.bfloat16), _NT,
                                    preferred_element_type=jnp.float32)
            st_ref[h] = st * jnp.exp2(b_last) + lax.dot_general(
                vb, (km * jnp.exp2(b_last - b_mid)).astype(jnp.bfloat16), _TN, preferred_element_type=jnp.float32)
            staged.append((attn, vb, inter))
        gates(nxt)
        keep = tri > 0
        for h, (attn, vb, inter) in enumerate(staged):
            attn = jnp.where(keep, attn.astype(jnp.bfloat16), 0)
            finish(h, jnp.dot(attn, vb, preferred_element_type=jnp.float32) + inter)

    def head_levels(h):
        cols = slice(h * HEAD, (h + 1) * HEAD)
        q = q_ref[:, cols]
        v = v_ref[:, cols]
        kk = key_ref[cur, :, cols]
        b = b_ref[cur, :, cols]
        attn = [jnp.zeros((SUBLANES, c), jnp.float32) for _ in range(ng)]

        for l in range(SMALL_LEVELS):
            half = 1 << l
            n = SUBLANES >> (l + 1)
            pieces = []
            for g in range(ng):
                cand = [brow(boundary(g * SUBLANES + (j << (l + 1)), half), cols) for j in range(n)]
                p = cand[-1]
                for j in range(n - 2, -1, -1):
                    p = jnp.where(sub < ((j + 1) << (l + 1)), cand[j], p)
                pieces.append(p)
            sgn = sgn_ref[l]
            fac = jnp.exp2((b - jnp.concatenate(pieces, axis=0)) * sgn)
            xl = (jnp.where(sgn > 0.0, q, kk) * fac).astype(jnp.bfloat16)
            zl = lax.dot_general(xl, xl, _NT, preferred_element_type=jnp.float32)
            for g in range(ng):
                rows = slice(g * SUBLANES, (g + 1) * SUBLANES)
                attn[g] = jnp.where(lev_ref[rows, :] == l, zl[rows], attn[g])

        for l in range(SMALL_LEVELS, SCAN_LEVELS):
            half = 1 << l
            xs, xq, q_groups = [], [], []
            for part in range(c // half):
                rows = slice(part * half, (part + 1) * half)
                bound = brow(boundary((part >> 1) * 2 * half, half), cols, half)
                if (part & 1) == qbit:
                    xp = (q[rows] * jnp.exp2(b[rows] - bound)).astype(jnp.bfloat16)
                    xq.append(xp)
                    q_groups += range(part * half // SUBLANES, (part + 1) * half // SUBLANES)
                else:
                    xp = (kk[rows] * jnp.exp2(bound - b[rows])).astype(jnp.bfloat16)
                xs.append(xp)
            zl = lax.dot_general(jnp.concatenate(xq, axis=0), jnp.concatenate(xs, axis=0), _NT,
                                 preferred_element_type=jnp.float32)
            for i, g in enumerate(q_groups):
                rows = slice(g * SUBLANES, (g + 1) * SUBLANES)
                attn[g] = jnp.where(lev_ref[rows, :] == l, zl[i * SUBLANES:(i + 1) * SUBLANES], attn[g])

        attn = jnp.concatenate(attn, axis=0).astype(jnp.bfloat16)
        o = jnp.dot(attn, v.astype(jnp.bfloat16), preferred_element_type=jnp.float32)
        o = o + jnp.sum(q * kk, axis=-1, keepdims=True) * v
        st = st_ref[h]
        qs = (q * jnp.exp2(b)).astype(jnp.bfloat16)
        o = o + lax.dot_general(qs, st.astype(jnp.bfloat16), _NT, preferred_element_type=jnp.float32)
        b_last = b_ref[cur, last:last + 1, cols]
        kh = (kk * jnp.exp2(b_last - b)).astype(jnp.bfloat16)
        st_ref[h] = st * jnp.exp2(b_last) + lax.dot_general(
            v.astype(jnp.bfloat16), kh, _TN, preferred_element_type=jnp.float32)
        finish(h, o)

    @pl.when(step > 0)
    def _():
        direct_ok = worst_ref[cur] >= -SAFE_DECAY * LOG2E

        @pl.when(direct_ok)
        def _():
            heads_direct()

        @pl.when(jnp.logical_not(direct_ok))
        def _():
            gates(nxt)
            for h in range(SCAN_HEADS):
                head_levels(h)


def _scan(proj, lbs, o_fwd, gain, out, tok0, n_seq, t):
    m = proj.shape[0]
    d = proj.shape[1] // 5
    c, gw = SCAN_CHUNK, SCAN_HEADS * HEAD
    assert t % c == 0 and tok0 % c == 0 and d % gw == 0
    nc, ng, cb0 = t // c, d // gw, tok0 // c
    final = o_fwd is not None

    def blk(which, lag=1):
        def index(s, g, i):
            ci = jnp.clip(i - lag, 0, nc - 1)
            return cb0 + s * nc + (nc - 1 - ci if final else ci), which * ng + g
        return pl.BlockSpec((c, gw), index)

    vec = pl.BlockSpec((1, gw), lambda s, g, i: (0, g))
    in_specs = [blk(0), blk(1), blk(3 if final else 2, lag=0), vec, vec]
    args = [proj, proj, proj, *lbs]
    if final:
        in_specs += [blk(4), blk(0), pl.BlockSpec((1, HEAD), lambda s, g, ci: (0, 0))]
        args += [proj, o_fwd, gain]
    aliases = {}
    if out is not None:
        aliases = {len(args): 0}
        in_specs.append(pl.BlockSpec(memory_space=pl.ANY))
        args.append(out)
    return pl.pallas_call(
        functools.partial(_scan_kernel, reverse=final, final=final),
        grid=(n_seq, ng, nc + 1),
        in_specs=in_specs,
        out_specs=blk(0),
        out_shape=jax.ShapeDtypeStruct((m, d), jnp.bfloat16 if final else jnp.float32),
        scratch_shapes=[pltpu.VMEM((2, c, gw), jnp.float32),
                        pltpu.VMEM((2, c, gw), jnp.float32),
                        pltpu.VMEM((SCAN_HEADS, HEAD, HEAD), jnp.float32),
                        pltpu.VMEM((c, c), jnp.int32),
                        pltpu.VMEM((SMALL_LEVELS, c, HEAD), jnp.float32),
                        pltpu.VMEM((c, c), jnp.bfloat16),
                        pltpu.SMEM((2,), jnp.float32)],
        input_output_aliases=aliases,
        compiler_params=_params("parallel", "parallel", "arbitrary"),
        name="hgrn_scan_bwd" if final else "hgrn_scan_fwd",
    )(*args)


def kernel(x_prompt, x_sample, ln_mix, ln_mlp, ln_final, w_qkv, rpb, w_o_a, w_in_b, lower_bounds, g_norm,
           w_o_b, w_up, w_down):
    d = x_prompt.shape[-1]
    depth = ln_mix.shape[0]
    groups = []
    tok = 0
    for a in (x_prompt, x_sample):
        groups.append((tok, a.shape[0], a.shape[1]))
        tok += a.shape[0] * a.shape[1]
    x = jnp.concatenate([x_prompt.reshape(-1, d), x_sample.reshape(-1, d)], axis=0)

    w_qkv, w_o_a, w_in_b, w_o_b, w_up, w_down = (_to_bf16(w) for w in (w_qkv, w_o_a, w_in_b, w_o_b, w_up, w_down))
    lb_sched, onem = _lb_schedule(lower_bounds)
    proj_tiles = (1024, 512)
    down_tiles = (2048, 1024, 1024)

    for layer in range(depth):
        j = layer // 2
        h = _rmsnorm(x, ln_mix[layer], jnp.bfloat16)
        if layer % 2 == 0:
            qkv = _matmul(h, w_qkv, j, jnp.bfloat16)
            bias_tab = _na_bias_table(rpb[j], HEAD ** -0.5)
            o = None
            for g in groups:
                o = _na_attention(qkv, bias_tab, o, *g)
            x, hb, ssq = _matmul_residual_norm(o, w_o_a, j, x, ln_mlp[layer], *proj_tiles)
        else:
            proj = _matmul(h, w_in_b, j, jnp.float32)
            lbs = [a[layer].reshape(1, d) for a in (lb_sched, onem)]
            gain = g_norm[j].reshape(1, HEAD).astype(jnp.float32)
            o_fwd = None
            for g in groups:
                o_fwd = _scan(proj, lbs, None, None, o_fwd, *g)
            o = None
            for g in groups:
                o = _scan(proj, lbs, o_fwd, gain, o, *g)
            x, hb, ssq = _matmul_residual_norm(o, w_o_b, j, x, ln_mlp[layer], *proj_tiles)
        u = _matmul(hb, w_up, layer, jnp.bfloat16, relu2=True, ssq=ssq)
        x = _matmul_residual(u, w_down, layer, x, *down_tiles)

    outs = []
    for (tok0, n_seq, t), a in zip(groups, (x_prompt, x_sample)):
        y = _rmsnorm(x, ln_final, jnp.float32, row0=tok0, rows=n_seq * t)
        outs.append(y.reshape(a.shape))
    return tuple(outs)
```

```python
import functools
import math

import numpy as np
import jax
import jax.numpy as jnp
from jax import lax
from jax.experimental import pallas as pl
from jax.experimental.pallas import tpu as pltpu

HEAD = 128
SUBLANES = 8
GRID_W = 64
NA_ROWS = 8
NA_KW = 16
NA_UNROLL = 32
SCAN_CHUNK = 128
SCAN_LEVELS = 7
SMALL_LEVELS = 3
SCAN_HEADS = 32
SAFE_DECAY = 60.0
RMS_EPS = 1e-6
NEG = -1e30
LOG2E = math.log2(math.e)
VMEM_LIMIT = 56 * 1024 * 1024

_NT = (((1,), (1,)), ((), ()))
_TN = (((0,), (0,)), ((), ()))


def _params(*sem):
    return pltpu.CompilerParams(dimension_semantics=sem, vmem_limit_bytes=VMEM_LIMIT)


def _tile(n, pref):
    if pref <= n and n % pref == 0:
        return pref
    t = 1
    while t * 2 <= min(n, pref) and n % (t * 2) == 0:
        t *= 2
    return t


def _lb_kernel(x_ref, lb_ref, onem_ref):
    x = x_ref[...]
    e = jnp.exp(x - jnp.max(x, axis=0, keepdims=True))
    sm = e / jnp.sum(e, axis=0, keepdims=True)
    depth = x.shape[0]
    c = sm[0:1]
    c0 = c
    for r in range(depth):
        if r:
            c = c + sm[r:r + 1]
        lb = c - c0
        lb_ref[r:r + 1, :] = lb
        onem_ref[r:r + 1, :] = 1.0 - lb


def _lb_schedule(lower_bounds):
    depth, d = lower_bounds.shape
    out = jax.ShapeDtypeStruct((depth, d), jnp.float32)
    return pl.pallas_call(_lb_kernel, out_shape=(out, out), name="lb_schedule")(
        lower_bounds.astype(jnp.float32))


def _rmsnorm_kernel(x_ref, g_ref, o_ref):
    x = x_ref[...]
    y = x * lax.rsqrt(jnp.mean(x * x, axis=-1, keepdims=True) + RMS_EPS)
    o_ref[...] = (y * g_ref[...]).astype(o_ref.dtype)


def _rmsnorm(x, g, out_dtype, row0=0, rows=None):
    m, d = x.shape
    rows = m if rows is None else rows
    tm = _tile(rows, 512)
    assert row0 % tm == 0
    off = row0 // tm
    return pl.pallas_call(
        _rmsnorm_kernel,
        grid=(rows // tm,),
        in_specs=[pl.BlockSpec((tm, d), lambda i: (i + off, 0)),
                  pl.BlockSpec((1, d), lambda i: (0, 0))],
        out_specs=pl.BlockSpec((tm, d), lambda i: (i, 0)),
        out_shape=jax.ShapeDtypeStruct((rows, d), out_dtype),
        compiler_params=_params("parallel"),
        name="rmsnorm",
    )(x, g.reshape(1, d).astype(jnp.float32))


def _mm_kernel(a_ref, w_ref, *rest, relu2, inv_d):
    o_ref = rest[-1]
    acc = jnp.dot(a_ref[...], w_ref[...], preferred_element_type=jnp.float32)
    if inv_d is not None:
        acc = acc * lax.rsqrt(rest[0][:, 0:1] * inv_d + RMS_EPS)
    if relu2:
        acc = jnp.maximum(acc, 0.0)
        acc = acc * acc
    o_ref[...] = acc.astype(o_ref.dtype)


def _mm_res_norm_kernel(a_ref, w_ref, r_ref, g_ref, o_ref, hb_ref, ssq_ref):
    @pl.when(pl.program_id(1) == 0)
    def _():
        ssq_ref[...] = jnp.zeros_like(ssq_ref)

    x = r_ref[...] + jnp.dot(a_ref[...], w_ref[...], preferred_element_type=jnp.float32)
    o_ref[...] = x
    hb_ref[...] = (x * g_ref[...]).astype(hb_ref.dtype)
    ssq_ref[...] += jnp.sum(x * x, axis=-1, keepdims=True)


def _mm_res_kernel(a_ref, w_ref, r_ref, o_ref, *, k_steps):
    if k_steps == 1:
        o_ref[...] = r_ref[...] + jnp.dot(a_ref[...], w_ref[...], preferred_element_type=jnp.float32)
    else:
        @pl.when(pl.program_id(2) == 0)
        def _():
            o_ref[...] = r_ref[...]

        o_ref[...] += jnp.dot(a_ref[...], w_ref[...], preferred_element_type=jnp.float32)


def _cast_kernel(x_ref, o_ref):
    o_ref[...] = x_ref[...].astype(o_ref.dtype)


def _to_bf16(w):
    l, kd, n = w.shape
    bk, bn = _tile(kd, 512), _tile(n, 4096)
    spec = pl.BlockSpec((None, bk, bn), lambda a, i, j: (a, i, j))
    return pl.pallas_call(
        _cast_kernel,
        grid=(l, kd // bk, n // bn),
        in_specs=[spec],
        out_specs=spec,
        out_shape=jax.ShapeDtypeStruct(w.shape, jnp.bfloat16),
        compiler_params=_params("parallel", "parallel", "parallel"),
        name="to_bf16",
    )(w)


def _matmul(a, w, layer, out_dtype, relu2=False, ssq=None):
    m, kd = a.shape
    n = w.shape[2]
    tm, tn = _tile(m, 1024), _tile(n, 1024)
    in_specs = [pl.BlockSpec((tm, kd), lambda i, j: (i, 0)),
                pl.BlockSpec((None, kd, tn), lambda i, j: (layer, 0, j))]
    args = [a, w]
    if ssq is not None:
        in_specs.append(pl.BlockSpec((tm, HEAD), lambda i, j: (i, 0)))
        args.append(ssq)
    return pl.pallas_call(
        functools.partial(_mm_kernel, relu2=relu2, inv_d=None if ssq is None else 1.0 / kd),
        grid=(m // tm, n // tn),
        in_specs=in_specs,
        out_specs=pl.BlockSpec((tm, tn), lambda i, j: (i, j)),
        out_shape=jax.ShapeDtypeStruct((m, n), out_dtype),
        compiler_params=_params("parallel", "parallel"),
        name="matmul_relu2" if relu2 else "matmul",
    )(*args)


def _matmul_residual_norm(a, w, layer, res, gain, tm, tn):
    m, kd = a.shape
    n = w.shape[2]
    tm, tn = _tile(m, tm), _tile(n, tn)
    tile = pl.BlockSpec((tm, tn), lambda i, j: (i, j))
    return pl.pallas_call(
        _mm_res_norm_kernel,
        grid=(m // tm, n // tn),
        in_specs=[pl.BlockSpec((tm, kd), lambda i, j: (i, 0)),
                  pl.BlockSpec((None, kd, tn), lambda i, j: (layer, 0, j)),
                  tile,
                  pl.BlockSpec((1, tn), lambda i, j: (0, j))],
        out_specs=[tile, tile, pl.BlockSpec((tm, HEAD), lambda i, j: (i, 0))],
        out_shape=[jax.ShapeDtypeStruct((m, n), jnp.float32),
                   jax.ShapeDtypeStruct((m, n), jnp.bfloat16),
                   jax.ShapeDtypeStruct((m, HEAD), jnp.float32)],
        input_output_aliases={2: 0},
        compiler_params=_params("parallel", "arbitrary"),
        name="matmul_residual_norm",
    )(a, w, res, gain.reshape(1, n).astype(jnp.float32))


def _matmul_residual(a, w, layer, res, tm, tn, tk):
    m, kd = a.shape
    n = w.shape[2]
    tm, tn, tk = _tile(m, tm), _tile(n, tn), _tile(kd, tk)
    return pl.pallas_call(
        functools.partial(_mm_res_kernel, k_steps=kd // tk),
        grid=(m // tm, n // tn, kd // tk),
        in_specs=[pl.BlockSpec((tm, tk), lambda i, j, k: (i, k)),
                  pl.BlockSpec((None, tk, tn), lambda i, j, k: (layer, k, j)),
                  pl.BlockSpec((tm, tn), lambda i, j, k: (i, j))],
        out_specs=pl.BlockSpec((tm, tn), lambda i, j, k: (i, j)),
        out_shape=jax.ShapeDtypeStruct((m, n), jnp.float32),
        input_output_aliases={2: 0},
        compiler_params=_params("parallel", "parallel", "arbitrary"),
        name="matmul_residual",
    )(a, w, res)


def _na_bias_table(rpb, scale):
    h = rpb.shape[0]
    w = GRID_W
    col = np.arange(w)
    cs = np.clip(col - NA_KW // 2, 0, w - NA_KW)
    dcol = col[None, :] - col[:, None] + (NA_KW - 1)
    col_ok = (col[None, :] >= cs[:, None]) & (col[None, :] < cs[:, None] + NA_KW)
    per_row = jnp.where(jnp.asarray(col_ok)[None, None],
                        rpb.astype(jnp.float32)[:, :, np.clip(dcol, 0, 2 * NA_KW - 2)] / scale,
                        NEG)
    ro = np.arange(NA_ROWS)[None, :] - np.arange(NA_ROWS)[:, None] + (NA_ROWS - 1)
    tab = per_row[:, ro]
    return tab.transpose(0, 1, 3, 2, 4).reshape(h, NA_ROWS, w, NA_ROWS * w)


def _na_kernel(q_ref, k_ref, v_ref, bias_ref, *rest, rows, scale):
    o_ref = rest[-1]
    w = GRID_W
    unroll = math.gcd(rows, NA_UNROLL)

    def body(i, carry):
        staged = []
        for u in range(unroll):
            r = i * unroll + u
            rs = jnp.clip(r - NA_ROWS // 2, 0, rows - NA_ROWS)
            qt = pl.multiple_of(r * w, w)
            kt = pl.multiple_of(rs * w, w)
            s = lax.dot_general(q_ref[pl.ds(qt, w), :], k_ref[pl.ds(kt, NA_ROWS * w), :], _NT,
                                preferred_element_type=jnp.float32)
            staged.append((s, r - rs, qt, kt))
        probs = []
        for s, i_row, qt, kt in staged:
            t = s + bias_ref[i_row]
            e = jnp.exp2((t - jnp.max(t, axis=-1, keepdims=True)) * (scale * LOG2E))
            probs.append((e.astype(jnp.bfloat16), jnp.sum(e, axis=-1, keepdims=True), qt, kt))
        for e, l, qt, kt in probs:
            o = jnp.dot(e, v_ref[pl.ds(kt, NA_ROWS * w), :], preferred_element_type=jnp.float32)
            o_ref[pl.ds(qt, w), :] = (o / l).astype(o_ref.dtype)
        return carry

    lax.fori_loop(0, rows // unroll, body, 0)


def _na_attention(qkv, bias_tab, out, tok0, n_seq, t):
    m = qkv.shape[0]
    d = qkv.shape[1] // 3
    nh = d // HEAD
    rows = t // GRID_W
    assert t % GRID_W == 0 and rows >= NA_ROWS and tok0 % t == 0
    s0 = tok0 // t
    scale = HEAD ** -0.5
    in_specs = [pl.BlockSpec((t, HEAD), lambda s, h: (s0 + s, h)),
                pl.BlockSpec((t, HEAD), lambda s, h: (s0 + s, nh + h)),
                pl.BlockSpec((t, HEAD), lambda s, h: (s0 + s, 2 * nh + h)),
                pl.BlockSpec((None, NA_ROWS, GRID_W, NA_ROWS * GRID_W), lambda s, h: (h, 0, 0, 0))]
    args = [qkv, qkv, qkv, bias_tab]
    aliases = {}
    if out is not None:
        in_specs.append(pl.BlockSpec(memory_space=pl.ANY))
        args.append(out)
        aliases = {4: 0}
    return pl.pallas_call(
        functools.partial(_na_kernel, rows=rows, scale=scale),
        grid=(n_seq, nh),
        in_specs=in_specs,
        out_specs=pl.BlockSpec((t, HEAD), lambda s, h: (s0 + s, h)),
        out_shape=jax.ShapeDtypeStruct((m, d), jnp.bfloat16),
        input_output_aliases=aliases,
        compiler_params=_params("parallel", "parallel"),
        name="na_attention",
    )(*args)


def _split3(x):
    hi = x.astype(jnp.bfloat16)
    r1 = x - hi.astype(jnp.float32)
    mid = r1.astype(jnp.bfloat16)
    lo = (r1 - mid.astype(jnp.float32)).astype(jnp.bfloat16)
    return hi, mid, lo


def _scan_kernel(q_ref, v_ref, z_ref, lb_ref, onem_ref, *rest, reverse, final):
    if final:
        g_ref, of_ref, gain_ref = rest[:3]
    o_ref, b_ref, key_ref, st_ref, lev_ref, sgn_ref, tri_ref, worst_ref = rest[-8:]
    c = SCAN_CHUNK
    ng = c // SUBLANES
    step = pl.program_id(2)
    qbit = 0 if reverse else 1
    last = 0 if reverse else c - 1
    mid = c // 2 if reverse else c // 2 - 1

    def gates(slot):
        z = z_ref[...]
        e = jnp.exp(-jnp.abs(z))
        pos = z >= 0.0
        lb = lb_ref[...]
        den = 1.0 + e
        log_f = jnp.log2(jnp.where(pos, 1.0 + lb * e, lb + e)) - jnp.log2(den)
        key_ref[slot] = onem_ref[...] * (jnp.where(pos, e, 1.0) / den)
        tri = tri_ref[...]
        hi, md, lo = _split3(log_f)
        b_all = (jnp.dot(tri, hi, preferred_element_type=jnp.float32)
                 + jnp.dot(tri, md, preferred_element_type=jnp.float32)
                 + jnp.dot(tri, lo, preferred_element_type=jnp.float32))
        b_ref[slot] = b_all
        b_mid = b_all[mid:mid + 1, :]
        half = jnp.minimum(b_mid, b_all[last:last + 1, :] - b_mid)
        worst_ref[slot] = jnp.min(half, axis=1, keepdims=True)[0, 0]

    @pl.when(step == 0)
    def _():
        st_ref[...] = jnp.zeros_like(st_ref)
        ti = lax.broadcasted_iota(jnp.int32, (c, c), 0)
        si = lax.broadcasted_iota(jnp.int32, (c, c), 1)
        tri_ref[...] = jnp.where((si >= ti) if reverse else (si <= ti), 1.0, 0.0).astype(jnp.bfloat16)
        x = ti ^ si
        lev = jnp.zeros((c, c), jnp.int32)
        for l in range(1, SCAN_LEVELS):
            lev = lev + jnp.where(x >= (1 << l), 1, 0)
        lev_ref[...] = jnp.where((si > ti) if reverse else (si < ti), lev, -1)
        row = lax.broadcasted_iota(jnp.int32, (c, HEAD), 0)
        for l in range(SMALL_LEVELS):
            sgn_ref[l] = jnp.where(((row >> l) & 1) == qbit, 1.0, -1.0)
        gates(0)

    sub = lax.broadcasted_iota(jnp.int32, (SUBLANES, HEAD), 0)
    cur = (step + 1) % 2
    nxt = step % 2

    def brow(r, cols, n=SUBLANES):
        return jnp.broadcast_to(b_ref[cur, r:r + 1, cols], (n, HEAD))

    def boundary(blk, half):
        return blk + (half if reverse else half - 1)

    def finish(h, o):
        cols = slice(h * HEAD, (h + 1) * HEAD)
        if final:
            o = o + of_ref[:, cols]
            o = o * lax.rsqrt(jnp.mean(o * o, axis=-1, keepdims=True) + RMS_EPS) * gain_ref[...]
            gt = g_ref[:, cols]
            o = o * (gt / (1.0 + jnp.exp(-gt)))
        o_ref[:, cols] = o.astype(o_ref.dtype)

    def heads_direct():
        tri = tri_ref[...]
        staged = []
        for h in range(SCAN_HEADS):
            cols = slice(h * HEAD, (h + 1) * HEAD)
            vb = v_ref[:, cols].astype(jnp.bfloat16)
            b_mid = b_ref[cur, mid:mid + 1, cols]
            b_last = b_ref[cur, last:last + 1, cols]
            d = b_ref[cur, :, cols] - b_mid
            qm = q_ref[:, cols] * jnp.exp2(d)
            km = key_ref[cur, :, cols] * jnp.exp2(-d)
            attn = lax.dot_general(qm.astype(jnp.bfloat16), km.astype(jnp.bfloat16), _NT,
                                   preferred_element_type=jnp.float32)
            st = st_ref[h]
            inter = lax.dot_general((qm * jnp.exp2(b_mid)).astype(jnp.bfloat16), st.astype(jnp.bfloat16), _NT,
                                    preferred_element_type=jnp.float32)
            st_ref[h] = st * jnp.exp2(b_last) + lax.dot_general(
                vb, (km * jnp.exp2(b_last - b_mid)).astype(jnp.bfloat16), _TN, preferred_element_type=jnp.float32)
            staged.append((attn, vb, inter))
        gates(nxt)
        keep = tri > 0
        for h, (attn, vb, inter) in enumerate(staged):
            attn = jnp.where(keep, attn.astype(jnp.bfloat16), 0)
            finish(h, jnp.dot(attn, vb, preferred_element_type=jnp.float32) + inter)

    def head_levels(h):
        cols = slice(h * HEAD, (h + 1) * HEAD)
        q = q_ref[:, cols]
        v = v_ref[:, cols]
        kk = key_ref[cur, :, cols]
        b = b_ref[cur, :, cols]
        attn = [jnp.zeros((SUBLANES, c), jnp.float32) for _ in range(ng)]

        for l in range(SMALL_LEVELS):
            half = 1 << l
            n = SUBLANES >> (l + 1)
            pieces = []
            for g in range(ng):
                cand = [brow(boundary(g * SUBLANES + (j << (l + 1)), half), cols) for j in range(n)]
                p = cand[-1]
                for j in range(n - 2, -1, -1):
                    p = jnp.where(sub < ((j + 1) << (l + 1)), cand[j], p)
                pieces.append(p)
            sgn = sgn_ref[l]
            fac = jnp.exp2((b - jnp.concatenate(pieces, axis=0)) * sgn)
            xl = (jnp.where(sgn > 0.0, q, kk) * fac).astype(jnp.bfloat16)
            zl = lax.dot_general(xl, xl, _NT, preferred_element_type=jnp.float32)
            for g in range(ng):
                rows = slice(g * SUBLANES, (g + 1) * SUBLANES)
                attn[g] = jnp.where(lev_ref[rows, :] == l, zl[rows], attn[g])

        for l in range(SMALL_LEVELS, SCAN_LEVELS):
            half = 1 << l
            xs, xq, q_groups = [], [], []
            for part in range(c // half):
                rows = slice(part * half, (part + 1) * half)
                bound = brow(boundary((part >> 1) * 2 * half, half), cols, half)
                if (part & 1) == qbit:
                    xp = (q[rows] * jnp.exp2(b[rows] - bound)).astype(jnp.bfloat16)
                    xq.append(xp)
                    q_groups += range(part * half // SUBLANES, (part + 1) * half // SUBLANES)
                else:
                    xp = (kk[rows] * jnp.exp2(bound - b[rows])).astype(jnp.bfloat16)
                xs.append(xp)
            zl = lax.dot_general(jnp.concatenate(xq, axis=0), jnp.concatenate(xs, axis=0), _NT,
                                 preferred_element_type=jnp.float32)
            for i, g in enumerate(q_groups):
                rows = slice(g * SUBLANES, (g + 1) * SUBLANES)
                attn[g] = jnp.where(lev_ref[rows, :] == l, zl[i * SUBLANES:(i + 1) * SUBLANES], attn[g])

        attn = jnp.concatenate(attn, axis=0).astype(jnp.bfloat16)
        o = jnp.dot(attn, v.astype(jnp.bfloat16), preferred_element_type=jnp.float32)
        o = o + jnp.sum(q * kk, axis=-1, keepdims=True) * v
        st = st_ref[h]
        qs = (q * jnp.exp2(b)).astype(jnp.bfloat16)
        o = o + lax.dot_general(qs, st.astype(jnp.bfloat16), _NT, preferred_element_type=jnp.float32)
        b_last = b_ref[cur, last:last + 1, cols]
        kh = (kk * jnp.exp2(b_last - b)).astype(jnp.bfloat16)
        st_ref[h] = st * jnp.exp2(b_last) + lax.dot_general(
            v.astype(jnp.bfloat16), kh, _TN, preferred_element_type=jnp.float32)
        finish(h, o)

    @pl.when(step > 0)
    def _():
        direct_ok = worst_ref[cur] >= -SAFE_DECAY * LOG2E

        @pl.when(direct_ok)
        def _():
            heads_direct()

        @pl.when(jnp.logical_not(direct_ok))
        def _():
            gates(nxt)
            for h in range(SCAN_HEADS):
                head_levels(h)


def _scan(proj, lbs, o_fwd, gain, out, tok0, n_seq, t):
    m = proj.shape[0]
    d = proj.shape[1] // 5
    c, gw = SCAN_CHUNK, SCAN_HEADS * HEAD
    assert t % c == 0 and tok0 % c == 0 and d % gw == 0
    nc, ng, cb0 = t // c, d // gw, tok0 // c
    final = o_fwd is not None

    def blk(which, lag=1):
        def index(s, g, i):
            ci = jnp.clip(i - lag, 0, nc - 1)
            return cb0 + s * nc + (nc - 1 - ci if final else ci), which * ng + g
        return pl.BlockSpec((c, gw), index)

    vec = pl.BlockSpec((1, gw), lambda s, g, i: (0, g))
    in_specs = [blk(0), blk(1), blk(3 if final else 2, lag=0), vec, vec]
    args = [proj, proj, proj, *lbs]
    if final:
        in_specs += [blk(4), blk(0), pl.BlockSpec((1, HEAD), lambda s, g, ci: (0, 0))]
        args += [proj, o_fwd, gain]
    aliases = {}
    if out is not None:
        aliases = {len(args): 0}
        in_specs.append(pl.BlockSpec(memory_space=pl.ANY))
        args.append(out)
    return pl.pallas_call(
        functools.partial(_scan_kernel, reverse=final, final=final),
        grid=(n_seq, ng, nc + 1),
        in_specs=in_specs,
        out_specs=blk(0),
        out_shape=jax.ShapeDtypeStruct((m, d), jnp.bfloat16 if final else jnp.float32),
        scratch_shapes=[pltpu.VMEM((2, c, gw), jnp.float32),
                        pltpu.VMEM((2, c, gw), jnp.float32),
                        pltpu.VMEM((SCAN_HEADS, HEAD, HEAD), jnp.float32),
                        pltpu.VMEM((c, c), jnp.int32),
                        pltpu.VMEM((SMALL_LEVELS, c, HEAD), jnp.float32),
                        pltpu.VMEM((c, c), jnp.bfloat16),
                        pltpu.SMEM((2,), jnp.float32)],
        input_output_aliases=aliases,
        compiler_params=_params("parallel", "parallel", "arbitrary"),
        name="hgrn_scan_bwd" if final else "hgrn_scan_fwd",
    )(*args)


def kernel(x_prompt, x_sample, ln_mix, ln_mlp, ln_final, w_qkv, rpb, w_o_a, w_in_b, lower_bounds, g_norm,
           w_o_b, w_up, w_down):
    d = x_prompt.shape[-1]
    depth = ln_mix.shape[0]
    groups = []
    tok = 0
    for a in (x_prompt, x_sample):
        groups.append((tok, a.shape[0], a.shape[1]))
        tok += a.shape[0] * a.shape[1]
    x = jnp.concatenate([x_prompt.reshape(-1, d), x_sample.reshape(-1, d)], axis=0)

    w_qkv, w_o_a, w_in_b, w_o_b, w_up, w_down = (_to_bf16(w) for w in (w_qkv, w_o_a, w_in_b, w_o_b, w_up, w_down))
    lb_sched, onem = _lb_schedule(lower_bounds)
    proj_tiles = (1024, 512)
    down_tiles = (2048, 1024, 1024)

    for layer in range(depth):
        j = layer // 2
        h = _rmsnorm(x, ln_mix[layer], jnp.bfloat16)
        if layer % 2 == 0:
            qkv = _matmul(h, w_qkv, j, jnp.bfloat16)
            bias_tab = _na_bias_table(rpb[j], HEAD ** -0.5)
            o = None
            for g in groups:
                o = _na_attention(qkv, bias_tab, o, *g)
            x, hb, ssq = _matmul_residual_norm(o, w_o_a, j, x, ln_mlp[layer], *proj_tiles)
        else:
            proj = _matmul(h, w_in_b, j, jnp.float32)
            lbs = [a[layer].reshape(1, d) for a in (lb_sched, onem)]
            gain = g_norm[j].reshape(1, HEAD).astype(jnp.float32)
            o_fwd = None
            for g in groups:
                o_fwd = _scan(proj, lbs, None, None, o_fwd, *g)
            o = None
            for g in groups:
                o = _scan(proj, lbs, o_fwd, gain, o, *g)
            x, hb, ssq = _matmul_residual_norm(o, w_o_b, j, x, ln_mlp[layer], *proj_tiles)
        u = _matmul(hb, w_up, layer, jnp.bfloat16, relu2=True, ssq=ssq)
        x = _matmul_residual(u, w_down, layer, x, *down_tiles)

    outs = []
    for (tok0, n_seq, t), a in zip(groups, (x_prompt, x_sample)):
        y = _rmsnorm(x, ln_final, jnp.float32, row0=tok0, rows=n_seq * t)
        outs.append(y.reshape(a.shape))
    return tuple(outs)
```

```python
import functools
import math

import numpy as np
import jax
import jax.numpy as jnp
from jax import lax
from jax.experimental import pallas as pl
from jax.experimental.pallas import tpu as pltpu

HEAD = 128
SUBLANES = 8
GRID_W = 64
NA_ROWS = 8
NA_KW = 16
NA_UNROLL = 32
SCAN_CHUNK = 128
SCAN_LEVELS = 7
SMALL_LEVELS = 3
SCAN_HEADS = 16
SAFE_DECAY = 60.0
RMS_EPS = 1e-6
NEG = -1e30
LOG2E = math.log2(math.e)
VMEM_LIMIT = 56 * 1024 * 1024

_NT = (((1,), (1,)), ((), ()))
_TN = (((0,), (0,)), ((), ()))


def _params(*sem):
    return pltpu.CompilerParams(dimension_semantics=sem, vmem_limit_bytes=VMEM_LIMIT)


def _tile(n, pref):
    if pref <= n and n % pref == 0:
        return pref
    t = 1
    while t * 2 <= min(n, pref) and n % (t * 2) == 0:
        t *= 2
    return t


def _lb_kernel(x_ref, lb_ref, onem_ref):
    x = x_ref[...]
    e = jnp.exp(x - jnp.max(x, axis=0, keepdims=True))
    sm = e / jnp.sum(e, axis=0, keepdims=True)
    depth = x.shape[0]
    c = sm[0:1]
    c0 = c
    for r in range(depth):
        if r:
            c = c + sm[r:r + 1]
        lb = c - c0
        lb_ref[r:r + 1, :] = lb
        onem_ref[r:r + 1, :] = 1.0 - lb


def _lb_schedule(lower_bounds):
    depth, d = lower_bounds.shape
    out = jax.ShapeDtypeStruct((depth, d), jnp.float32)
    return pl.pallas_call(_lb_kernel, out_shape=(out, out), name="lb_schedule")(
        lower_bounds.astype(jnp.float32))


def _rmsnorm_kernel(x_ref, g_ref, *rest):
    o_ref = rest[-1]
    x = x_ref[...]
    y = x * lax.rsqrt(jnp.mean(x * x, axis=-1, keepdims=True) + RMS_EPS)
    o_ref[...] = (y * g_ref[...]).astype(o_ref.dtype)


def _rmsnorm(x, g, out_dtype, row0=0, rows=None, out_rows=None, out_row0=0, prev=None):
    m, d = x.shape
    rows = m if rows is None else rows
    out_rows = rows if out_rows is None else out_rows
    tm = _tile(rows, 512)
    assert row0 % tm == 0 and out_row0 % tm == 0
    off, out_off = row0 // tm, out_row0 // tm
    in_specs = [pl.BlockSpec((tm, d), lambda i: (i + off, 0)),
                pl.BlockSpec((1, d), lambda i: (0, 0))]
    args = [x, g.reshape(1, d).astype(jnp.float32)]
    if prev is not None:
        in_specs.append(pl.BlockSpec(memory_space=pl.ANY))
        args.append(prev)
    return pl.pallas_call(
        _rmsnorm_kernel,
        grid=(rows // tm,),
        in_specs=in_specs,
        out_specs=pl.BlockSpec((tm, d), lambda i: (i + out_off, 0)),
        out_shape=jax.ShapeDtypeStruct((out_rows, d), out_dtype),
        input_output_aliases={} if prev is None else {2: 0},
        compiler_params=_params("parallel"),
        name="rmsnorm",
    )(*args)


def _mm_kernel(a_ref, w_ref, *rest, relu2, inv_d):
    o_ref = rest[-1]
    acc = jnp.dot(a_ref[...], w_ref[...], preferred_element_type=jnp.float32)
    if inv_d is not None:
        acc = acc * lax.rsqrt(rest[0][:, 0:1] * inv_d + RMS_EPS)
    if relu2:
        acc = jnp.maximum(acc, 0.0)
        acc = acc * acc
    o_ref[...] = acc.astype(o_ref.dtype)


def _mm_res_norm_kernel(a_ref, w_ref, r_ref, g_ref, *rest):
    o_ref, hb_ref, ssq_ref = rest[-3:]
    @pl.when(pl.program_id(1) == 0)
    def _():
        ssq_ref[...] = jnp.zeros_like(ssq_ref)

    x = r_ref[...] + jnp.dot(a_ref[...], w_ref[...], preferred_element_type=jnp.float32)
    o_ref[...] = x
    hb_ref[...] = (x * g_ref[...]).astype(hb_ref.dtype)
    ssq_ref[...] += jnp.sum(x * x, axis=-1, keepdims=True)


def _mm_res_kernel(a_ref, w_ref, r_ref, o_ref, *, k_steps):
    if k_steps == 1:
        o_ref[...] = r_ref[...] + jnp.dot(a_ref[...], w_ref[...], preferred_element_type=jnp.float32)
    else:
        @pl.when(pl.program_id(2) == 0)
        def _():
            o_ref[...] = r_ref[...]

        o_ref[...] += jnp.dot(a_ref[...], w_ref[...], preferred_element_type=jnp.float32)


def _cast_kernel(x_ref, o_ref):
    o_ref[...] = x_ref[...].astype(o_ref.dtype)


def _to_bf16(w):
    l, kd, n = w.shape
    bk, bn = _tile(kd, 512), _tile(n, 4096)
    spec = pl.BlockSpec((None, bk, bn), lambda a, i, j: (a, i, j))
    return pl.pallas_call(
        _cast_kernel,
        grid=(l, kd // bk, n // bn),
        in_specs=[spec],
        out_specs=spec,
        out_shape=jax.ShapeDtypeStruct(w.shape, jnp.bfloat16),
        compiler_params=_params("parallel", "parallel", "parallel"),
        name="to_bf16",
    )(w)


def _matmul(a, w, layer, out_dtype, relu2=False, ssq=None):
    m, kd = a.shape
    n = w.shape[2]
    tm, tn = _tile(m, 1024), _tile(n, 1024)
    in_specs = [pl.BlockSpec((tm, kd), lambda i, j: (i, 0)),
                pl.BlockSpec((None, kd, tn), lambda i, j: (layer, 0, j))]
    args = [a, w]
    if ssq is not None:
        in_specs.append(pl.BlockSpec((tm, HEAD), lambda i, j: (i, 0)))
        args.append(ssq)
    return pl.pallas_call(
        functools.partial(_mm_kernel, relu2=relu2, inv_d=None if ssq is None else 1.0 / kd),
        grid=(m // tm, n // tn),
        in_specs=in_specs,
        out_specs=pl.BlockSpec((tm, tn), lambda i, j: (i, j)),
        out_shape=jax.ShapeDtypeStruct((m, n), out_dtype),
        compiler_params=_params("parallel", "parallel"),
        name="matmul_relu2" if relu2 else "matmul",
    )(*args)


def _matmul_residual_norm(a, w, layer, res, gain, tm, tn, row0=0, prev=None):
    m, kd = a.shape
    rows, n = res.shape
    tm, tn = _tile(rows, tm), _tile(n, tn)
    assert row0 % tm == 0
    off = row0 // tm
    tile = pl.BlockSpec((tm, tn), lambda i, j: (i + off, j))
    in_specs = [pl.BlockSpec((tm, kd), lambda i, j: (i + off, 0)),
                pl.BlockSpec((None, kd, tn), lambda i, j: (layer, 0, j)),
                pl.BlockSpec((tm, tn), lambda i, j: (i, j)),
                pl.BlockSpec((1, tn), lambda i, j: (0, j))]
    args = [a, w, res, gain.reshape(1, n).astype(jnp.float32)]
    aliases = {2: 0} if rows == m else {}
    if prev is not None:
        in_specs += [pl.BlockSpec(memory_space=pl.ANY)] * 3
        args += list(prev)
        aliases = {4: 0, 5: 1, 6: 2}
    return pl.pallas_call(
        _mm_res_norm_kernel,
        grid=(rows // tm, n // tn),
        in_specs=in_specs,
        out_specs=[tile, tile, pl.BlockSpec((tm, HEAD), lambda i, j: (i + off, 0))],
        out_shape=[jax.ShapeDtypeStruct((m, n), jnp.float32),
                   jax.ShapeDtypeStruct((m, n), jnp.bfloat16),
                   jax.ShapeDtypeStruct((m, HEAD), jnp.float32)],
        input_output_aliases=aliases,
        compiler_params=_params("parallel", "arbitrary"),
        name="matmul_residual_norm",
    )(*args)


def _matmul_residual(a, w, layer, res, tm, tn, tk):
    m, kd = a.shape
    n = w.shape[2]
    tm, tn, tk = _tile(m, tm), _tile(n, tn), _tile(kd, tk)
    return pl.pallas_call(
        functools.partial(_mm_res_kernel, k_steps=kd // tk),
        grid=(m // tm, n // tn, kd // tk),
        in_specs=[pl.BlockSpec((tm, tk), lambda i, j, k: (i, k)),
                  pl.BlockSpec((None, tk, tn), lambda i, j, k: (layer, k, j)),
                  pl.BlockSpec((tm, tn), lambda i, j, k: (i, j))],
        out_specs=pl.BlockSpec((tm, tn), lambda i, j, k: (i, j)),
        out_shape=jax.ShapeDtypeStruct((m, n), jnp.float32),
        input_output_aliases={2: 0},
        compiler_params=_params("parallel", "parallel", "arbitrary"),
        name="matmul_residual",
    )(a, w, res)


def _na_bias_table(rpb, scale):
    h = rpb.shape[0]
    w = GRID_W
    col = np.arange(w)
    cs = np.clip(col - NA_KW // 2, 0, w - NA_KW)
    dcol = col[None, :] - col[:, None] + (NA_KW - 1)
    col_ok = (col[None, :] >= cs[:, None]) & (col[None, :] < cs[:, None] + NA_KW)
    per_row = jnp.where(jnp.asarray(col_ok)[None, None],
                        rpb.astype(jnp.float32)[:, :, np.clip(dcol, 0, 2 * NA_KW - 2)] / scale,
                        NEG)
    ro = np.arange(NA_ROWS)[None, :] - np.arange(NA_ROWS)[:, None] + (NA_ROWS - 1)
    tab = per_row[:, ro]
    return tab.transpose(0, 1, 3, 2, 4).reshape(h, NA_ROWS, w, NA_ROWS * w)


def _na_kernel(q_ref, k_ref, v_ref, bias_ref, *rest, rows, scale):
    o_ref = rest[-1]
    w = GRID_W
    unroll = math.gcd(rows, NA_UNROLL)

    def body(i, carry):
        staged = []
        for u in range(unroll):
            r = i * unroll + u
            rs = jnp.clip(r - NA_ROWS // 2, 0, rows - NA_ROWS)
            qt = pl.multiple_of(r * w, w)
            kt = pl.multiple_of(rs * w, w)
            s = lax.dot_general(q_ref[pl.ds(qt, w), :], k_ref[pl.ds(kt, NA_ROWS * w), :], _NT,
                                preferred_element_type=jnp.float32)
            staged.append((s, r - rs, qt, kt))
        probs = []
        for s, i_row, qt, kt in staged:
            t = s + bias_ref[i_row]
            e = jnp.exp2((t - jnp.max(t, axis=-1, keepdims=True)) * (scale * LOG2E))
            probs.append((e.astype(jnp.bfloat16), jnp.sum(e, axis=-1, keepdims=True), qt, kt))
        for e, l, qt, kt in probs:
            o = jnp.dot(e, v_ref[pl.ds(kt, NA_ROWS * w), :], preferred_element_type=jnp.float32)
            o_ref[pl.ds(qt, w), :] = (o / l).astype(o_ref.dtype)
        return carry

    lax.fori_loop(0, rows // unroll, body, 0)


def _na_attention(qkv, bias_tab, out, tok0, n_seq, t):
    m = qkv.shape[0]
    d = qkv.shape[1] // 3
    nh = d // HEAD
    rows = t // GRID_W
    assert t % GRID_W == 0 and rows >= NA_ROWS and tok0 % t == 0
    s0 = tok0 // t
    scale = HEAD ** -0.5
    in_specs = [pl.BlockSpec((t, HEAD), lambda s, h: (s0 + s, h)),
                pl.BlockSpec((t, HEAD), lambda s, h: (s0 + s, nh + h)),
                pl.BlockSpec((t, HEAD), lambda s, h: (s0 + s, 2 * nh + h)),
                pl.BlockSpec((None, NA_ROWS, GRID_W, NA_ROWS * GRID_W), lambda s, h: (h, 0, 0, 0))]
    args = [qkv, qkv, qkv, bias_tab]
    aliases = {}
    if out is not None:
        in_specs.append(pl.BlockSpec(memory_space=pl.ANY))
        args.append(out)
        aliases = {4: 0}
    return pl.pallas_call(
        functools.partial(_na_kernel, rows=rows, scale=scale),
        grid=(n_seq, nh),
        in_specs=in_specs,
        out_specs=pl.BlockSpec((t, HEAD), lambda s, h: (s0 + s, h)),
        out_shape=jax.ShapeDtypeStruct((m, d), jnp.bfloat16),
        input_output_aliases=aliases,
        compiler_params=_params("parallel", "parallel"),
        name="na_attention",
    )(*args)


def _split3(x):
    hi = x.astype(jnp.bfloat16)
    r1 = x - hi.astype(jnp.float32)
    mid = r1.astype(jnp.bfloat16)
    lo = (r1 - mid.astype(jnp.float32)).astype(jnp.bfloat16)
    return hi, mid, lo


def _scan_kernel(q_ref, v_ref, z_ref, lb_ref, onem_ref, *rest, reverse, final):
    if final:
        g_ref, of_ref, gain_ref = rest[:3]
    o_ref, b_ref, key_ref, st_ref, lev_ref, sgn_ref, tri_ref, worst_ref = rest[-8:]
    c = SCAN_CHUNK
    ng = c // SUBLANES
    step = pl.program_id(2)
    qbit = 0 if reverse else 1
    last = 0 if reverse else c - 1
    mid = c // 2 if reverse else c // 2 - 1

    def gates(slot):
        z = z_ref[...]
        e = jnp.exp(-jnp.abs(z))
        pos = z >= 0.0
        lb = lb_ref[...]
        den = 1.0 + e
        log_f = jnp.log2(jnp.where(pos, 1.0 + lb * e, lb + e)) - jnp.log2(den)
        key_ref[slot] = onem_ref[...] * (jnp.where(pos, e, 1.0) / den)
        tri = tri_ref[...]
        hi, md, lo = _split3(log_f)
        b_all = (jnp.dot(tri, hi, preferred_element_type=jnp.float32)
                 + jnp.dot(tri, md, preferred_element_type=jnp.float32)
                 + jnp.dot(tri, lo, preferred_element_type=jnp.float32))
        b_ref[slot] = b_all
        b_mid = b_all[mid:mid + 1, :]
        half = jnp.minimum(b_mid, b_all[last:last + 1, :] - b_mid)
        worst_ref[slot] = jnp.min(half, axis=1, keepdims=True)[0, 0]

    @pl.when(step == 0)
    def _():
        st_ref[...] = jnp.zeros_like(st_ref)
        ti = lax.broadcasted_iota(jnp.int32, (c, c), 0)
        si = lax.broadcasted_iota(jnp.int32, (c, c), 1)
        tri_ref[...] = jnp.where((si >= ti) if reverse else (si <= ti), 1.0, 0.0).astype(jnp.bfloat16)
        x = ti ^ si
        lev = jnp.zeros((c, c), jnp.int32)
        for l in range(1, SCAN_LEVELS):
            lev = lev + jnp.where(x >= (1 << l), 1, 0)
        lev_ref[...] = jnp.where((si > ti) if reverse else (si < ti), lev, -1)
        row = lax.broadcasted_iota(jnp.int32, (c, HEAD), 0)
        for l in range(SMALL_LEVELS):
            sgn_ref[l] = jnp.where(((row >> l) & 1) == qbit, 1.0, -1.0)
        gates(0)

    sub = lax.broadcasted_iota(jnp.int32, (SUBLANES, HEAD), 0)
    cur = (step + 1) % 2
    nxt = step % 2

    def brow(r, cols, n=SUBLANES):
        return jnp.broadcast_to(b_ref[cur, r:r + 1, cols], (n, HEAD))

    def boundary(blk, half):
        return blk + (half if reverse else half - 1)

    def finish(h, o):
        cols = slice(h * HEAD, (h + 1) * HEAD)
        if final:
            o = o + of_ref[:, cols]
            o = o * lax.rsqrt(jnp.mean(o * o, axis=-1, keepdims=True) + RMS_EPS) * gain_ref[...]
            gt = g_ref[:, cols]
            o = o * (gt / (1.0 + jnp.exp(-gt)))
        o_ref[:, cols] = o.astype(o_ref.dtype)

    def heads_direct():
        tri = tri_ref[...]
        staged = []
        for h in range(SCAN_HEADS):
            cols = slice(h * HEAD, (h + 1) * HEAD)
            vb = v_ref[:, cols].astype(jnp.bfloat16)
            b_mid = b_ref[cur, mid:mid + 1, cols]
            b_last = b_ref[cur, last:last + 1, cols]
            d = b_ref[cur, :, cols] - b_mid
            qm = q_ref[:, cols] * jnp.exp2(d)
            km = key_ref[cur, :, cols] * jnp.exp2(-d)
            attn = lax.dot_general(qm.astype(jnp.bfloat16), km.astype(jnp.bfloat16), _NT,
                                   preferred_element_type=jnp.float32)
            st = st_ref[h]
            inter = lax.dot_general((qm * jnp.exp2(b_mid)).astype(jnp.bfloat16), st.astype(jnp.bfloat16), _NT,
                                    preferred_element_type=jnp.float32)
            st_ref[h] = st * jnp.exp2(b_last) + lax.dot_general(
                vb, (km * jnp.exp2(b_last - b_mid)).astype(jnp.bfloat16), _TN, preferred_element_type=jnp.float32)
            staged.append((attn, vb, inter))
        gates(nxt)
        keep = tri > 0
        for h, (attn, vb, inter) in enumerate(staged):
            attn = jnp.where(keep, attn.astype(jnp.bfloat16), 0)
            finish(h, jnp.dot(attn, vb, preferred_element_type=jnp.float32) + inter)

    def head_levels(h):
        cols = slice(h * HEAD, (h + 1) * HEAD)
        q = q_ref[:, cols]
        v = v_ref[:, cols]
        kk = key_ref[cur, :, cols]
        b = b_ref[cur, :, cols]
        attn = [jnp.zeros((SUBLANES, c), jnp.float32) for _ in range(ng)]

        for l in range(SMALL_LEVELS):
            half = 1 << l
            n = SUBLANES >> (l + 1)
            pieces = []
            for g in range(ng):
                cand = [brow(boundary(g * SUBLANES + (j << (l + 1)), half), cols) for j in range(n)]
                p = cand[-1]
                for j in range(n - 2, -1, -1):
                    p = jnp.where(sub < ((j + 1) << (l + 1)), cand[j], p)
                pieces.append(p)
            sgn = sgn_ref[l]
            fac = jnp.exp2((b - jnp.concatenate(pieces, axis=0)) * sgn)
            xl = (jnp.where(sgn > 0.0, q, kk) * fac).astype(jnp.bfloat16)
            zl = lax.dot_general(xl, xl, _NT, preferred_element_type=jnp.float32)
            for g in range(ng):
                rows = slice(g * SUBLANES, (g + 1) * SUBLANES)
                attn[g] = jnp.where(lev_ref[rows, :] == l, zl[rows], attn[g])

        for l in range(SMALL_LEVELS, SCAN_LEVELS):
            half = 1 << l
            xs, xq, q_groups = [], [], []
            for part in range(c // half):
                rows = slice(part * half, (part + 1) * half)
                bound = brow(boundary((part >> 1) * 2 * half, half), cols, half)
                if (part & 1) == qbit:
                    xp = (q[rows] * jnp.exp2(b[rows] - bound)).astype(jnp.bfloat16)
                    xq.append(xp)
                    q_groups += range(part * half // SUBLANES, (part + 1) * half // SUBLANES)
                else:
                    xp = (kk[rows] * jnp.exp2(bound - b[rows])).astype(jnp.bfloat16)
                xs.append(xp)
            zl = lax.dot_general(jnp.concatenate(xq, axis=0), jnp.concatenate(xs, axis=0), _NT,
                                 preferred_element_type=jnp.float32)
            for i, g in enumerate(q_groups):
                rows = slice(g * SUBLANES, (g + 1) * SUBLANES)
                attn[g] = jnp.where(lev_ref[rows, :] == l, zl[i * SUBLANES:(i + 1) * SUBLANES], attn[g])

        attn = jnp.concatenate(attn, axis=0).astype(jnp.bfloat16)
        o = jnp.dot(attn, v.astype(jnp.bfloat16), preferred_element_type=jnp.float32)
        o = o + jnp.sum(q * kk, axis=-1, keepdims=True) * v
        st = st_ref[h]
        qs = (q * jnp.exp2(b)).astype(jnp.bfloat16)
        o = o + lax.dot_general(qs, st.astype(jnp.bfloat16), _NT, preferred_element_type=jnp.float32)
        b_last = b_ref[cur, last:last + 1, cols]
        kh = (kk * jnp.exp2(b_last - b)).astype(jnp.bfloat16)
        st_ref[h] = st * jnp.exp2(b_last) + lax.dot_general(
            v.astype(jnp.bfloat16), kh, _TN, preferred_element_type=jnp.float32)
        finish(h, o)

    @pl.when(step > 0)
    def _():
        direct_ok = worst_ref[cur] >= -SAFE_DECAY * LOG2E

        @pl.when(direct_ok)
        def _():
            heads_direct()

        @pl.when(jnp.logical_not(direct_ok))
        def _():
            gates(nxt)
            for h in range(SCAN_HEADS):
                head_levels(h)


def _scan(proj, lbs, o_fwd, gain, out, tok0, n_seq, t):
    m = proj.shape[0]
    d = proj.shape[1] // 5
    c, gw = SCAN_CHUNK, SCAN_HEADS * HEAD
    assert t % c == 0 and tok0 % c == 0 and d % gw == 0
    nc, ng, cb0 = t // c, d // gw, tok0 // c
    final = o_fwd is not None

    def blk(which, lag=1):
        def index(s, g, i):
            ci = jnp.clip(i - lag, 0, nc - 1)
            return cb0 + s * nc + (nc - 1 - ci if final else ci), which * ng + g
        return pl.BlockSpec((c, gw), index)

    vec = pl.BlockSpec((1, gw), lambda s, g, i: (0, g))
    in_specs = [blk(0), blk(1), blk(3 if final else 2, lag=0), vec, vec]
    args = [proj, proj, proj, *lbs]
    if final:
        in_specs += [blk(4), blk(0), pl.BlockSpec((1, HEAD), lambda s, g, ci: (0, 0))]
        args += [proj, o_fwd, gain]
    aliases = {}
    if out is not None:
        aliases = {len(args): 0}
        in_specs.append(pl.BlockSpec(memory_space=pl.ANY))
        args.append(out)
    return pl.pallas_call(
        functools.partial(_scan_kernel, reverse=final, final=final),
        grid=(n_seq, ng, nc + 1),
        in_specs=in_specs,
        out_specs=blk(0),
        out_shape=jax.ShapeDtypeStruct((m, d), jnp.bfloat16 if final else jnp.float32),
        scratch_shapes=[pltpu.VMEM((2, c, gw), jnp.float32),
                        pltpu.VMEM((2, c, gw), jnp.float32),
                        pltpu.VMEM((SCAN_HEADS, HEAD, HEAD), jnp.float32),
                        pltpu.VMEM((c, c), jnp.int32),
                        pltpu.VMEM((SMALL_LEVELS, c, HEAD), jnp.float32),
                        pltpu.VMEM((c, c), jnp.bfloat16),
                        pltpu.SMEM((2,), jnp.float32)],
        input_output_aliases=aliases,
        compiler_params=_params("parallel", "parallel", "arbitrary"),
        name="hgrn_scan_bwd" if final else "hgrn_scan_fwd",
    )(*args)


def kernel(x_prompt, x_sample, ln_mix, ln_mlp, ln_final, w_qkv, rpb, w_o_a, w_in_b, lower_bounds, g_norm,
           w_o_b, w_up, w_down):
    d = x_prompt.shape[-1]
    depth = ln_mix.shape[0]
    groups = []
    tok = 0
    for a in (x_prompt, x_sample):
        groups.append((tok, a.shape[0], a.shape[1]))
        tok += a.shape[0] * a.shape[1]
    streams = [a.reshape(-1, d) for a in (x_prompt, x_sample)]
    m = tok
    x = None

    w_qkv, w_o_a, w_in_b, w_o_b, w_up, w_down = (_to_bf16(w) for w in (w_qkv, w_o_a, w_in_b, w_o_b, w_up, w_down))
    lb_sched, onem = _lb_schedule(lower_bounds)
    proj_tiles = (1024, 512)
    down_tiles = (2048, 1024, 1024)

    for layer in range(depth):
        j = layer // 2
        if x is None:
            h = None
            for xs, (tok0, _, _) in zip(streams, groups):
                h = _rmsnorm(xs, ln_mix[layer], jnp.bfloat16, out_rows=m, out_row0=tok0, prev=h)
        else:
            h = _rmsnorm(x, ln_mix[layer], jnp.bfloat16)
        if layer % 2 == 0:
            qkv = _matmul(h, w_qkv, j, jnp.bfloat16)
            bias_tab = _na_bias_table(rpb[j], HEAD ** -0.5)
            o = None
            for g in groups:
                o = _na_attention(qkv, bias_tab, o, *g)
            if x is None:
                res = None
                for xs, (tok0, _, _) in zip(streams, groups):
                    res = _matmul_residual_norm(o, w_o_a, j, xs, ln_mlp[layer], *proj_tiles, row0=tok0, prev=res)
                x, hb, ssq = res
            else:
                x, hb, ssq = _matmul_residual_norm(o, w_o_a, j, x, ln_mlp[layer], *proj_tiles)
        else:
            proj = _matmul(h, w_in_b, j, jnp.float32)
            lbs = [a[layer].reshape(1, d) for a in (lb_sched, onem)]
            gain = g_norm[j].reshape(1, HEAD).astype(jnp.float32)
            o_fwd = None
            for g in groups:
                o_fwd = _scan(proj, lbs, None, None, o_fwd, *g)
            o = None
            for g in groups:
                o = _scan(proj, lbs, o_fwd, gain, o, *g)
            x, hb, ssq = _matmul_residual_norm(o, w_o_b, j, x, ln_mlp[layer], *proj_tiles)
        u = _matmul(hb, w_up, layer, jnp.bfloat16, relu2=True, ssq=ssq)
        x = _matmul_residual(u, w_down, layer, x, *down_tiles)

    outs = []
    for (tok0, n_seq, t), a in zip(groups, (x_prompt, x_sample)):
        y = _rmsnorm(x, ln_final, jnp.float32, row0=tok0, rows=n_seq * t)
        outs.append(y.reshape(a.shape))
    return tuple(outs)
```
